```python
import jax, jax.numpy as jnp
from jax import lax
import numpy as np

D_MODEL = 1024
BATCH = 8
SEQ = 8192
DEPTH = 2
DEC_BATCH = 16
DEC_SEQ = 4096
PAST_LEN = 128

GRID_W = 64
N_BRANCH = 3

NA_HEADS = 8
NA_HEAD_DIM = 64
NA_WIDTH = NA_HEADS * NA_HEAD_DIM
NA_WIN_ROWS = 8
NA_WIN_COLS = 16
NA_COL_BLOCK = 16
NA_KEY_COLS = 2 * NA_COL_BLOCK
NA_RPB_ROWS = 2 * NA_WIN_ROWS - 1
NA_RPB_COLS = 2 * NA_WIN_COLS - 1

MLA_HEADS = 8
MLA_NOPE = 64
MLA_ROPE = 32
MLA_QK = MLA_NOPE + MLA_ROPE
MLA_V = 64
MLA_WIDTH = MLA_HEADS * MLA_V
MLA_Q_RANK = 256
MLA_KV_RANK = 128
MLA_Q_BLOCK = 128
ROPE_THETA = 10000.0

RW_HEADS = 8
RW_HEAD_DIM = 64
RW_WIDTH = RW_HEADS * RW_HEAD_DIM
RW_DECAY_RANK = 64
RW_A_RANK = 64
RW_G_RANK = 128
RW_LN_EPS = 64e-5
RW_IN = 3 * RW_WIDTH + 2 * RW_DECAY_RANK + 2 * RW_A_RANK + RW_G_RANK
RW_SPLITS = (RW_WIDTH, 2 * RW_WIDTH, 3 * RW_WIDTH, 3 * RW_WIDTH + 2 * RW_DECAY_RANK, 3 * RW_WIDTH + 2 * RW_DECAY_RANK + 2 * RW_A_RANK)

D_FF = ((8 * D_MODEL // 3 + 255) // 256) * 256

IN_SIZES = (NA_WIDTH, NA_WIDTH, NA_WIDTH, MLA_Q_RANK, MLA_KV_RANK, MLA_ROPE, RW_IN, N_BRANCH * D_MODEL)
D_IN = sum(IN_SIZES)
IN_SPLITS = tuple(int(s) for s in np.cumsum(IN_SIZES)[:-1])
NORM_EPS = 1e-6

kernel_name = 'hybrid_na_mla_rwkv7_encoder'


def rms_norm(x, g):
    xf = x.astype(jnp.float32)
    y = xf * lax.rsqrt(jnp.mean(xf * xf, axis=-1, keepdims=True) + NORM_EPS)
    return (y * g.astype(jnp.float32)).astype(x.dtype)


def axial_rope(seq_len):
    t = jnp.arange(seq_len, dtype=jnp.int32)
    row = (t // GRID_W).astype(jnp.float32)
    col = (t % GRID_W).astype(jnp.float32)
    n_freq = MLA_ROPE // 4
    inv_freq = ROPE_THETA ** (-jnp.arange(n_freq, dtype=jnp.float32) / n_freq)
    ang = jnp.concatenate([row[:, None] * inv_freq, col[:, None] * inv_freq], axis=-1)
    return jnp.cos(ang), jnp.sin(ang)


def apply_rope(x, cos, sin):
    half = x.shape[-1] // 2
    xf = x.astype(jnp.float32)
    x1, x2 = xf[..., :half], xf[..., half:]
    c = cos[None, :, None, :]
    s = sin[None, :, None, :]
    return jnp.concatenate([x1 * c - x2 * s, x1 * s + x2 * c], axis=-1).astype(x.dtype)


def neighbourhood_attention(q, k, v, rpb):
    b, L, h, dh = q.shape
    rows = L // GRID_W
    wr = min(NA_WIN_ROWS, rows)
    ncb = GRID_W // NA_COL_BLOCK
    qg = q.reshape(b, rows, ncb, NA_COL_BLOCK, h, dh)
    kg = k.reshape(b, rows, GRID_W, h, dh)
    vg = v.reshape(b, rows, GRID_W, h, dh)
    j = np.arange(ncb)
    kc_start = np.clip(j * NA_COL_BLOCK - NA_WIN_COLS // 2, 0, GRID_W - NA_KEY_COLS)
    key_cols = kc_start[:, None] + np.arange(NA_KEY_COLS)[None, :]
    q_cols = j[:, None] * NA_COL_BLOCK + np.arange(NA_COL_BLOCK)[None, :]
    win_start = np.clip(q_cols - NA_WIN_COLS // 2, 0, GRID_W - NA_WIN_COLS)
    kc = key_cols[:, None, :]
    col_ok = (kc >= win_start[:, :, None]) & (kc < win_start[:, :, None] + NA_WIN_COLS)
    dc_idx = np.clip(kc - q_cols[:, :, None] + NA_WIN_COLS - 1, 0, NA_RPB_COLS - 1)
    col_bias = rpb.astype(jnp.float32)[:, :, dc_idx]
    col_mask = jnp.asarray(col_ok)[None, None, :, :, None, :]
    scale = dh ** -0.5

    def one_row(r):
        rs = jnp.clip(r - wr // 2, 0, rows - wr)
        k_rows = lax.dynamic_slice_in_dim(kg, rs, wr, axis=1)
        v_rows = lax.dynamic_slice_in_dim(vg, rs, wr, axis=1)
        k_blk = k_rows[:, :, key_cols]
        v_blk = v_rows[:, :, key_cols]
        q_row = lax.dynamic_index_in_dim(qg, r, axis=1, keepdims=False)
        s = jnp.einsum('bjqhd,bwjkhd->bhjqwk', q_row, k_blk, preferred_element_type=jnp.float32) * scale
        dr_idx = rs + jnp.arange(wr) - r + NA_WIN_ROWS - 1
        bias = jnp.transpose(jnp.take(col_bias, dr_idx, axis=1), (0, 2, 3, 1, 4))
        s = jnp.where(col_mask, s + bias[None], -jnp.inf)
        p = jax.nn.softmax(s.reshape(b, h, ncb, NA_COL_BLOCK, wr * NA_KEY_COLS), axis=-1)
        p = p.reshape(s.shape).astype(v.dtype)
        o = jnp.einsum('bhjqwk,bwjkhd->bjqhd', p, v_blk)
        return o.reshape(b, GRID_W, h * dh)

    out = lax.map(one_row, jnp.arange(rows))
    return jnp.transpose(out, (1, 0, 2, 3)).reshape(b, L, h * dh)


def mla_attention(cq, ckv, kr, cq_norm, ckv_norm, w_uq, w_ukv, q_norm, k_norm, cos, sin):
    b, L, _ = cq.shape
    q = (rms_norm(cq, cq_norm) @ w_uq).reshape(b, L, MLA_HEADS, MLA_QK)
    kv = (rms_norm(ckv, ckv_norm) @ w_ukv).reshape(b, L, MLA_HEADS, MLA_NOPE + MLA_V)
    k_nope, v = kv[..., :MLA_NOPE], kv[..., MLA_NOPE:]
    k = jnp.concatenate([k_nope, jnp.broadcast_to(kr[:, :, None, :], (b, L, MLA_HEADS, MLA_ROPE))], axis=-1)
    q = rms_norm(q, q_norm)
    k = rms_norm(k, k_norm)
    q = jnp.concatenate([q[..., :MLA_NOPE], apply_rope(q[..., MLA_NOPE:], cos, sin)], axis=-1)
    k = jnp.concatenate([k[..., :MLA_NOPE], apply_rope(k[..., MLA_NOPE:], cos, sin)], axis=-1)
    nblk = L // MLA_Q_BLOCK
    qb = jnp.transpose(q.reshape(b, nblk, MLA_Q_BLOCK, MLA_HEADS, MLA_QK), (1, 0, 2, 3, 4))
    scale = MLA_QK ** -0.5

    def one_block(qi):
        s = jnp.einsum('bqhd,bkhd->bhqk', qi, k, preferred_element_type=jnp.float32) * scale
        p = jax.nn.softmax(s, axis=-1).astype(v.dtype)
        return jnp.einsum('bhqk,bkhd->bqhd', p, v)

    o = lax.map(one_block, qb)
    return jnp.transpose(o, (1, 0, 2, 3, 4)).reshape(b, L, MLA_WIDTH)


def centred_shift(p):
    prev = jnp.pad(p[:, :-1], ((0, 0), (1, 0), (0, 0)))
    nxt = jnp.pad(p[:, 1:], ((0, 0), (0, 1), (0, 0)))
    return 0.5 * (prev + nxt)


def to_heads(t):
    return t.reshape(t.shape[:-1] + (RW_HEADS, RW_HEAD_DIM))


def rwkv7_scan(r, w, k, v, kk, a, reverse):
    b, L, h, n = r.shape

    def step(S, inp):
        r_t, w_t, k_t, v_t, kk_t, a_t = inp
        s_kk = jnp.einsum('bhvk,bhk->bhv', S, kk_t)
        S = S * w_t[:, :, None, :] - s_kk[..., None] * (kk_t * a_t)[:, :, None, :] + v_t[..., None] * k_t[:, :, None, :]
        return S, jnp.einsum('bhvk,bhk->bhv', S, r_t)

    xs = tuple(jnp.moveaxis(t, 1, 0) for t in (r, w, k, v, kk, a))
    s0 = jnp.zeros((b, h, n, n), jnp.float32)
    _, ys = lax.scan(step, s0, xs, reverse=reverse)
    return jnp.moveaxis(ys, 0, 1)


def rwkv7_mix(p, mu, w0, w_up, a0, a_up, g_up, k_k, k_a, r_k, ln_w, ln_b):
    f32 = jnp.float32
    b, L, _ = p.shape
    p = p.astype(f32)
    p = p + mu.astype(f32) * (centred_shift(p) - p)
    r, k, v, wd, ad, gd = jnp.split(p, RW_SPLITS, axis=-1)
    wd = jnp.tanh(wd).reshape(b, L, 2, RW_DECAY_RANK)
    ad = ad.reshape(b, L, 2, RW_A_RANK)
    w_raw = w0.astype(f32) + jnp.einsum('bldr,drc->bldc', wd, w_up.astype(f32))
    decay = jnp.exp(-jnp.exp(-jax.nn.softplus(-w_raw) - 0.5))
    a = jax.nn.sigmoid(a0.astype(f32) + jnp.einsum('bldr,drc->bldc', ad, a_up.astype(f32)))
    g = jax.nn.sigmoid(gd) @ g_up.astype(f32)
    kk = to_heads(k * k_k.astype(f32))
    kk = kk * lax.rsqrt(jnp.sum(kk * kk, axis=-1, keepdims=True) + 1e-12)
    kd = k[:, :, None, :] * (1.0 + (a - 1.0) * k_a.astype(f32))
    rh, vh = to_heads(r), to_heads(v)
    y = (rwkv7_scan(rh, to_heads(decay[:, :, 0]), to_heads(kd[:, :, 0]), vh, kk, to_heads(a[:, :, 0]), False)
         + rwkv7_scan(rh, to_heads(decay[:, :, 1]), to_heads(kd[:, :, 1]), vh, kk, to_heads(a[:, :, 1]), True))
    mean = jnp.mean(y, axis=-1, keepdims=True)
    var = jnp.mean(jnp.square(y - mean), axis=-1, keepdims=True)
    y = ((y - mean) * lax.rsqrt(var + RW_LN_EPS)).reshape(b, L, RW_WIDTH) * ln_w.astype(f32) + ln_b.astype(f32)
    bonus = jnp.sum(rh[:, :, None] * to_heads(kd) * r_k.astype(f32), axis=(2, -1))
    y = y + (bonus[..., None] * vh).reshape(b, L, RW_WIDTH)
    return y * g


def encoder_trunk(x, norm1_g, w_in, b_gate, na_q_norm, na_k_norm, na_rpb, na_proj,
                  mla_cq_norm, mla_ckv_norm, mla_w_uq, mla_w_ukv, mla_q_norm, mla_k_norm, mla_proj,
                  rw_mu, rw_w0, rw_w_up, rw_a0, rw_a_up, rw_g_up, rw_k_k, rw_k_a, rw_r_k, rw_ln_w, rw_ln_b, rw_proj,
                  w_out, norm2_g, ffn_w_gate, ffn_w_up, ffn_w_down):
    b, L, d = x.shape
    cos, sin = axial_rope(L)
    for l in range(DEPTH):
        h = rms_norm(x, norm1_g[l])
        proj = h @ w_in[l]
        qa, ka, va, cq, ckv, kr, rw_cols, gate_in = jnp.split(proj, IN_SPLITS, axis=-1)
        qa = rms_norm(qa.reshape(b, L, NA_HEADS, NA_HEAD_DIM), na_q_norm[l])
        ka = rms_norm(ka.reshape(b, L, NA_HEADS, NA_HEAD_DIM), na_k_norm[l])
        va = va.reshape(b, L, NA_HEADS, NA_HEAD_DIM)
        y_a = neighbourhood_attention(qa, ka, va, na_rpb[l]) @ na_proj[l]
        y_b = mla_attention(cq, ckv, kr, mla_cq_norm[l], mla_ckv_norm[l], mla_w_uq[l], mla_w_ukv[l],
                            mla_q_norm[l], mla_k_norm[l], cos, sin) @ mla_proj[l]
        y_c = rwkv7_mix(rw_cols, rw_mu[l], rw_w0[l], rw_w_up[l], rw_a0[l], rw_a_up[l], rw_g_up[l],
                        rw_k_k[l], rw_k_a[l], rw_r_k[l], rw_ln_w[l], rw_ln_b[l]).astype(x.dtype) @ rw_proj[l]
        gates = jax.nn.sigmoid((gate_in + b_gate[l]).astype(jnp.float32)).astype(x.dtype).reshape(b, L, N_BRANCH, d)
        mixed = gates[:, :, 0] * y_a + gates[:, :, 1] * y_b + gates[:, :, 2] * y_c
        x = x + mixed @ w_out[l]
        h2 = rms_norm(x, norm2_g[l])
        x = x + (jax.nn.silu(h2 @ ffn_w_gate[l]) * (h2 @ ffn_w_up[l])) @ ffn_w_down[l]
    return x


def setup_inputs(seed: int = 0) -> dict:
    key = jax.random.key(seed)
    ks = iter(jax.random.split(key, 48))

    def nrm(shape, scale):
        return scale * jax.random.normal(next(ks), shape, dtype=jnp.float32)

    def gain(shape):
        return 1.0 + nrm(shape, 0.05)

    return {
        'x_prompt': nrm((BATCH, SEQ, D_MODEL), 1.0),
        'x_sample': nrm((DEC_BATCH, DEC_SEQ, D_MODEL), 1.0),
        'norm1_g': gain((DEPTH, D_MODEL)),
        'w_in': nrm((DEPTH, D_MODEL, D_IN), D_MODEL ** -0.5),
        'b_gate': nrm((DEPTH, N_BRANCH * D_MODEL), 0.1),
        'na_q_norm': gain((DEPTH, NA_HEAD_DIM)),
        'na_k_norm': gain((DEPTH, NA_HEAD_DIM)),
        'na_rpb': nrm((DEPTH, NA_HEADS, NA_RPB_ROWS, NA_RPB_COLS), 0.5),
        'na_proj': nrm((DEPTH, NA_WIDTH, D_MODEL), NA_WIDTH ** -0.5),
        'mla_cq_norm': gain((DEPTH, MLA_Q_RANK)),
        'mla_ckv_norm': gain((DEPTH, MLA_KV_RANK)),
        'mla_w_uq': nrm((DEPTH, MLA_Q_RANK, MLA_HEADS * MLA_QK), MLA_Q_RANK ** -0.5),
        'mla_w_ukv': nrm((DEPTH, MLA_KV_RANK, MLA_HEADS * (MLA_NOPE + MLA_V)), MLA_KV_RANK ** -0.5),
        'mla_q_norm': gain((DEPTH, MLA_QK)),
        'mla_k_norm': gain((DEPTH, MLA_QK)),
        'mla_proj': nrm((DEPTH, MLA_WIDTH, D_MODEL), MLA_WIDTH ** -0.5),
        'rw_mu': jax.random.uniform(next(ks), (DEPTH, RW_IN), dtype=jnp.float32),
        'rw_w0': nrm((DEPTH, 2, RW_WIDTH), 1.0),
        'rw_w_up': nrm((DEPTH, 2, RW_DECAY_RANK, RW_WIDTH), 0.5 * RW_DECAY_RANK ** -0.5),
        'rw_a0': nrm((DEPTH, 2, RW_WIDTH), 0.5),
        'rw_a_up': nrm((DEPTH, 2, RW_A_RANK, RW_WIDTH), 0.5 * RW_A_RANK ** -0.5),
        'rw_g_up': nrm((DEPTH, RW_G_RANK, RW_WIDTH), RW_G_RANK ** -0.5),
        'rw_k_k': 0.85 + nrm((DEPTH, RW_WIDTH), 0.05),
        'rw_k_a': gain((DEPTH, RW_WIDTH)),
        'rw_r_k': nrm((DEPTH, RW_HEADS, RW_HEAD_DIM), 0.1),
        'rw_ln_w': gain((DEPTH, RW_WIDTH)),
        'rw_ln_b': nrm((DEPTH, RW_WIDTH), 0.02),
        'rw_proj': nrm((DEPTH, RW_WIDTH, D_MODEL), RW_WIDTH ** -0.5),
        'w_out': nrm((DEPTH, D_MODEL, D_MODEL), D_MODEL ** -0.5),
        'norm2_g': gain((DEPTH, D_MODEL)),
        'ffn_w_gate': nrm((DEPTH, D_MODEL, D_FF), D_MODEL ** -0.5),
        'ffn_w_up': nrm((DEPTH, D_MODEL, D_FF), D_MODEL ** -0.5),
        'ffn_w_down': nrm((DEPTH, D_FF, D_MODEL), D_FF ** -0.5),
    }


def reference(x_prompt, x_sample, norm1_g, w_in, b_gate, na_q_norm, na_k_norm, na_rpb, na_proj,
              mla_cq_norm, mla_ckv_norm, mla_w_uq, mla_w_ukv, mla_q_norm, mla_k_norm, mla_proj,
              rw_mu, rw_w0, rw_w_up, rw_a0, rw_a_up, rw_g_up, rw_k_k, rw_k_a, rw_r_k, rw_ln_w, rw_ln_b, rw_proj,
              w_out, norm2_g, ffn_w_gate, ffn_w_up, ffn_w_down):
    weights = (norm1_g, w_in, b_gate, na_q_norm, na_k_norm, na_rpb, na_proj,
               mla_cq_norm, mla_ckv_norm, mla_w_uq, mla_w_ukv, mla_q_norm, mla_k_norm, mla_proj,
               rw_mu, rw_w0, rw_w_up, rw_a0, rw_a_up, rw_g_up, rw_k_k, rw_k_a, rw_r_k, rw_ln_w, rw_ln_b, rw_proj,
               w_out, norm2_g, ffn_w_gate, ffn_w_up, ffn_w_down)
    y_prompt = encoder_trunk(x_prompt, *weights)
    y_sample = encoder_trunk(x_sample, *weights)
    return (y_prompt, y_sample)
```

```python
import functools

import jax
import jax.numpy as jnp
import numpy as np
from jax import lax
from jax.experimental import pallas as pl
from jax.experimental.pallas import tpu as pltpu

F32 = jnp.float32
BF16 = jnp.bfloat16

D_MODEL = 1024
GRID_W = 64
N_BRANCH = 3
NORM_EPS = 1e-6

NA_HEADS = 8
NA_HEAD_DIM = 64
NA_WIDTH = NA_HEADS * NA_HEAD_DIM
NA_WIN_ROWS = 8
NA_WIN_COLS = 16
NA_RPB_ROWS = 2 * NA_WIN_ROWS - 1
NA_RPB_COLS = 2 * NA_WIN_COLS - 1
NA_MASK = -1e30

MLA_HEADS = 8
MLA_NOPE = 64
MLA_ROPE = 32
MLA_QK = MLA_NOPE + MLA_ROPE
MLA_V = 64
MLA_WIDTH = MLA_HEADS * MLA_V
MLA_Q_RANK = 256
MLA_KV_RANK = 128
MLA_HEAD_PAD = 128
ROPE_THETA = 10000.0

RW_HEADS = 8
RW_HEAD_DIM = 64
RW_WIDTH = RW_HEADS * RW_HEAD_DIM
RW_DECAY_RANK = 64
RW_A_RANK = 64
RW_G_RANK = 128
RW_LN_EPS = 64e-5
RW_IN = 3 * RW_WIDTH + 2 * RW_DECAY_RANK + 2 * RW_A_RANK + RW_G_RANK

D_FF = 2816

VMEM_LIMIT_BYTES = 56 * 1024 * 1024
LANES = 128


def _params(*sem):
    return pltpu.CompilerParams(dimension_semantics=sem, vmem_limit_bytes=VMEM_LIMIT_BYTES)


def _tile(n, pref, mult=8):
    if n <= pref:
        return n
    t = (pref // mult) * mult
    while t >= mult:
        if n % t == 0:
            return t
        t -= mult
    return n


def _norm_matmul_kernel(x_ref, g_ref, w_ref, o_ref, h_ref):
    @pl.when(pl.program_id(1) == 0)
    def _():
        x = x_ref[...]
        ms = jnp.mean(x * x, axis=-1, keepdims=True)
        h_ref[...] = (x * lax.rsqrt(ms + NORM_EPS) * g_ref[...]).astype(BF16)

    o_ref[...] = jnp.dot(h_ref[...], w_ref[...], preferred_element_type=F32).astype(o_ref.dtype)


def norm_matmul(x, g, w, out_dtype, tm_pref=1024, tn_pref=1024):
    t, d = x.shape
    n = w.shape[1]
    tm = _tile(t, tm_pref)
    tn = _tile(n, tn_pref, LANES)
    return pl.pallas_call(
        _norm_matmul_kernel,
        grid=(t // tm, n // tn),
        in_specs=[
            pl.BlockSpec((tm, d), lambda i, j: (i, 0)),
            pl.BlockSpec((1, d), lambda i, j: (0, 0)),
            pl.BlockSpec((d, tn), lambda i, j: (0, j)),
        ],
        out_specs=pl.BlockSpec((tm, tn), lambda i, j: (i, j)),
        out_shape=jax.ShapeDtypeStruct((t, n), out_dtype),
        scratch_shapes=[pltpu.VMEM((tm, d), BF16)],
        compiler_params=_params("parallel", "arbitrary"),
        name="norm_matmul",
    )(x, g.reshape(1, d), w)


def _na_prep_kernel(x_ref, gq_ref, gk_ref, q_ref, k_ref, v_ref):
    for h in range(NA_HEADS):
        lo = h * NA_HEAD_DIM
        q = x_ref[0, :, lo:lo + NA_HEAD_DIM].astype(F32)
        k = x_ref[0, :, NA_WIDTH + lo:NA_WIDTH + lo + NA_HEAD_DIM].astype(F32)
        q = q * lax.rsqrt(jnp.mean(q * q, axis=-1, keepdims=True) + NORM_EPS) * gq_ref[...]
        k = k * lax.rsqrt(jnp.mean(k * k, axis=-1, keepdims=True) + NORM_EPS) * gk_ref[...]
        q_ref[0, h] = q.astype(BF16)
        k_ref[0, h] = k.astype(BF16)
        v_ref[0, h] = x_ref[0, :, 2 * NA_WIDTH + lo:2 * NA_WIDTH + lo + NA_HEAD_DIM]


def na_prep(attn_in, gq, gk, b, L):
    tm = _tile(L, 512)
    x3 = attn_in.reshape(b, L, attn_in.shape[-1])
    hm = jax.ShapeDtypeStruct((b, NA_HEADS, L, NA_HEAD_DIM), BF16)
    hm_spec = pl.BlockSpec((1, NA_HEADS, tm, NA_HEAD_DIM), lambda i, j: (i, 0, j, 0))
    return pl.pallas_call(
        _na_prep_kernel,
        grid=(b, L // tm),
        in_specs=[
            pl.BlockSpec((1, tm, 3 * NA_WIDTH), lambda i, j: (i, j, 0)),
            pl.BlockSpec((1, NA_HEAD_DIM), lambda i, j: (0, 0)),
            pl.BlockSpec((1, NA_HEAD_DIM), lambda i, j: (0, 0)),
        ],
        out_specs=[hm_spec, hm_spec, hm_spec],
        out_shape=[hm, hm, hm],
        compiler_params=_params("parallel", "parallel"),
        name="na_prep",
    )(x3, gq.reshape(1, -1), gk.reshape(1, -1))


NA_ROWS_PER_STEP = 8
NA_BLOCK = NA_ROWS_PER_STEP * GRID_W
NA_KEYS = NA_WIN_ROWS * GRID_W


def _na_attn_kernel(q_ref, kp_ref, kc_ref, kn_ref, vp_ref, vc_ref, vn_ref, bias_ref, o_ref, kbuf, vbuf, *, rows):
    i = pl.program_id(1)
    kbuf[:, 0:NA_BLOCK] = kp_ref[0]
    kbuf[:, NA_BLOCK:2 * NA_BLOCK] = kc_ref[0]
    kbuf[:, 2 * NA_BLOCK:3 * NA_BLOCK] = kn_ref[0]
    vbuf[:, 0:NA_BLOCK] = vp_ref[0]
    vbuf[:, NA_BLOCK:2 * NA_BLOCK] = vc_ref[0]
    vbuf[:, 2 * NA_BLOCK:3 * NA_BLOCK] = vn_ref[0]

    def row_body(j, carry):
        r = i * NA_ROWS_PER_STEP + j
        rs = jnp.clip(r - NA_WIN_ROWS // 2, 0, rows - NA_WIN_ROWS)
        off = pl.multiple_of((rs - (i - 1) * NA_ROWS_PER_STEP) * GRID_W, GRID_W)
        pat = r - rs
        qoff = pl.multiple_of(j * GRID_W, GRID_W)
        for h in range(NA_HEADS):
            q = q_ref[0, h, pl.ds(qoff, GRID_W), :]
            k = kbuf[h, pl.ds(off, NA_KEYS), :]
            v = vbuf[h, pl.ds(off, NA_KEYS), :]
            s = lax.dot_general(q, k, (((1,), (1,)), ((), ())), preferred_element_type=F32)
            s = s + bias_ref[pat, h]
            m = jnp.max(s, axis=-1, keepdims=True)
            p = jnp.exp(s - m)
            l = jnp.sum(p, axis=-1, keepdims=True)
            o = jnp.dot(p.astype(BF16), v, preferred_element_type=F32) / l
            o_ref[0, pl.ds(qoff, GRID_W), h * NA_HEAD_DIM:(h + 1) * NA_HEAD_DIM] = o.astype(o_ref.dtype)
        return carry

    lax.fori_loop(0, NA_ROWS_PER_STEP, row_body, 0)


def _na_bias_table(rpb):
    pat = np.arange(NA_WIN_ROWS)[:, None]
    w = np.arange(NA_WIN_ROWS)[None, :]
    dr_idx = w - pat + NA_WIN_ROWS - 1
    qc = np.arange(GRID_W)[:, None]
    kc = np.arange(GRID_W)[None, :]
    win_start = np.clip(qc - NA_WIN_COLS // 2, 0, GRID_W - NA_WIN_COLS)
    ok = (kc >= win_start) & (kc < win_start + NA_WIN_COLS)
    dc_idx = np.clip(kc - qc + NA_WIN_COLS - 1, 0, NA_RPB_COLS - 1)
    t = rpb.astype(F32)[:, dr_idx][:, :, :, dc_idx]
    t = jnp.where(jnp.asarray(ok)[None, None, None], t, NA_MASK)
    t = jnp.transpose(t, (1, 0, 3, 2, 4))
    return t.reshape(NA_WIN_ROWS, NA_HEADS, GRID_W, NA_KEYS)


def na_attention(q, k, v, bias):
    b, _, L, _ = q.shape
    rows = L // GRID_W
    assert rows % NA_ROWS_PER_STEP == 0 and rows >= NA_WIN_ROWS
    nblk = rows // NA_ROWS_PER_STEP
    blk = (1, NA_HEADS, NA_BLOCK, NA_HEAD_DIM)
    cur = pl.BlockSpec(blk, lambda bi, i: (bi, 0, i, 0))
    prev = pl.BlockSpec(blk, lambda bi, i: (bi, 0, jnp.maximum(i - 1, 0), 0))
    nxt = pl.BlockSpec(blk, lambda bi, i: (bi, 0, jnp.minimum(i + 1, nblk - 1), 0))
    buf = pltpu.VMEM((NA_HEADS, 3 * NA_BLOCK, NA_HEAD_DIM), BF16)
    return pl.pallas_call(
        functools.partial(_na_attn_kernel, rows=rows),
        grid=(b, nblk),
        in_specs=[cur, prev, cur, nxt, prev, cur, nxt,
                  pl.BlockSpec(bias.shape, lambda bi, i: (0, 0, 0, 0))],
        out_specs=pl.BlockSpec((1, NA_BLOCK, NA_WIDTH), lambda bi, i: (bi, i, 0)),
        out_shape=jax.ShapeDtypeStruct((b, L, NA_WIDTH), BF16),
        scratch_shapes=[buf, buf],
        compiler_params=_params("parallel", "parallel"),
        name="na_attn",
    )(q, k, k, k, v, v, v, bias)


def _rope_tables(L):
    t = np.arange(L)
    row = (t // GRID_W).astype(np.float32)
    col = (t % GRID_W).astype(np.float32)
    n_freq = MLA_ROPE // 4
    inv_freq = jnp.asarray(ROPE_THETA, F32) ** (-jnp.arange(n_freq, dtype=F32) / n_freq)
    ang = jnp.concatenate([jnp.asarray(row)[:, None] * inv_freq, jnp.asarray(col)[:, None] * inv_freq], axis=-1)
    cos, sin = jnp.cos(ang), jnp.sin(ang)
    half = MLA_ROPE // 2
    ones = jnp.ones((L, MLA_NOPE), F32)
    zpad = jnp.zeros((L, MLA_HEAD_PAD - MLA_QK), F32)
    znope = jnp.zeros((L, MLA_NOPE), F32)
    zhalf = jnp.zeros((L, half), F32)
    c_tab = jnp.concatenate([ones, cos, cos, zpad], axis=-1)
    s1_tab = jnp.concatenate([znope, -sin, zhalf, zpad], axis=-1)
    s2_tab = jnp.concatenate([znope, zhalf, sin, zpad], axis=-1)
    return c_tab, s1_tab, s2_tab


def _mla_prep_kernel(x_ref, gcq_ref, gckv_ref, wuq_ref, wuk_ref, wuv_ref, gq_ref, gk_ref,
                     c_ref, s1_ref, s2_ref, q_ref, k_ref, v_ref):
    half = MLA_ROPE // 2
    cq = x_ref[0, :, 0:MLA_Q_RANK].astype(F32)
    ckv = x_ref[0, :, MLA_Q_RANK:MLA_Q_RANK + MLA_KV_RANK].astype(F32)
    kr = x_ref[0, :, MLA_Q_RANK + MLA_KV_RANK:].astype(F32)
    cqn = (cq * lax.rsqrt(jnp.mean(cq * cq, axis=-1, keepdims=True) + NORM_EPS) * gcq_ref[...]).astype(BF16)
    ckvn = (ckv * lax.rsqrt(jnp.mean(ckv * ckv, axis=-1, keepdims=True) + NORM_EPS) * gckv_ref[...]).astype(BF16)
    q_raw = jnp.dot(cqn, wuq_ref[...], preferred_element_type=F32)
    k_raw = jnp.dot(ckvn, wuk_ref[...], preferred_element_type=F32)
    v_ref[0] = jnp.dot(ckvn, wuv_ref[...], preferred_element_type=F32).astype(BF16)
    c, s1, s2 = c_ref[...], s1_ref[...], s2_ref[...]

    def norm_rope(t, g):
        t = t * lax.rsqrt(jnp.sum(t * t, axis=-1, keepdims=True) * (1.0 / MLA_QK) + NORM_EPS) * g
        return t * c + pltpu.roll(t, MLA_HEAD_PAD - half, 1) * s1 + pltpu.roll(t, half, 1) * s2

    for h in range(MLA_HEADS):
        sl = slice(h * MLA_HEAD_PAD, (h + 1) * MLA_HEAD_PAD)
        q_ref[0, :, sl] = norm_rope(q_raw[:, sl], gq_ref[...]).astype(BF16)
        k_ref[0, :, sl] = norm_rope(k_raw[:, sl] + kr, gk_ref[...]).astype(BF16)


def mla_prep(attn_in, lw, tabs, b, L):
    tm = _tile(L, 512)
    width = MLA_HEADS * MLA_HEAD_PAD
    x3 = attn_in.reshape(b, L, attn_in.shape[-1])
    in_w = MLA_Q_RANK + MLA_KV_RANK + MLA_HEAD_PAD
    col_blk = (3 * NA_WIDTH) // in_w
    assert col_blk * in_w == 3 * NA_WIDTH

    def full(a):
        return pl.BlockSpec(a.shape, lambda i, j: (0,) * a.ndim)

    tab_spec = pl.BlockSpec((tm, MLA_HEAD_PAD), lambda i, j: (j, 0))
    weights = (lw["gcq"], lw["gckv"], lw["wuq"], lw["wuk"], lw["wuv"], lw["gq"], lw["gk"])
    return pl.pallas_call(
        _mla_prep_kernel,
        grid=(b, L // tm),
        in_specs=[pl.BlockSpec((1, tm, in_w), lambda i, j: (i, j, col_blk))]
        + [full(a) for a in weights] + [tab_spec] * 3,
        out_specs=[pl.BlockSpec((1, tm, width), lambda i, j: (i, j, 0)),
                   pl.BlockSpec((1, tm, width), lambda i, j: (i, j, 0)),
                   pl.BlockSpec((1, tm, MLA_WIDTH), lambda i, j: (i, j, 0))],
        out_shape=[jax.ShapeDtypeStruct((b, L, width), BF16),
                   jax.ShapeDtypeStruct((b, L, width), BF16),
                   jax.ShapeDtypeStruct((b, L, MLA_WIDTH), BF16)],
        compiler_params=_params("parallel", "parallel"),
        name="mla_prep",
    )(x3, *weights, *tabs)


def _mla_flash_kernel(q_ref, k_ref, v_ref, o_ref, m_ref, l_ref, acc_ref):
    ki = pl.program_id(2)

    @pl.when(ki == 0)
    def _():
        m_ref[...] = jnp.full(m_ref.shape, -jnp.inf, F32)
        l_ref[...] = jnp.zeros(l_ref.shape, F32)
        acc_ref[...] = jnp.zeros(acc_ref.shape, F32)

    for h in range(MLA_HEADS):
        q = q_ref[0, :, h * MLA_HEAD_PAD:(h + 1) * MLA_HEAD_PAD]
        k = k_ref[0, :, h * MLA_HEAD_PAD:(h + 1) * MLA_HEAD_PAD]
        v2 = v_ref[0, :, (h // 2) * LANES:(h // 2 + 1) * LANES]
        s = lax.dot_general(q, k, (((1,), (1,)), ((), ())), preferred_element_type=F32)
        m_old = m_ref[h]
        m_new = jnp.maximum(m_old, jnp.max(s, axis=-1, keepdims=True))
        alpha = jnp.exp(m_old - m_new)
        p = jnp.exp(s - m_new)
        l_ref[h] = alpha * l_ref[h] + jnp.sum(p, axis=-1, keepdims=True)
        acc_ref[h] = alpha * acc_ref[h] + jnp.dot(p.astype(BF16), v2, preferred_element_type=F32)
        m_ref[h] = m_new

    @pl.when(ki == pl.num_programs(2) - 1)
    def _():
        lane = lax.broadcasted_iota(jnp.int32, (acc_ref.shape[1], LANES), 1)
        for hp in range(MLA_HEADS // 2):
            even = acc_ref[2 * hp] / l_ref[2 * hp]
            odd = acc_ref[2 * hp + 1] / l_ref[2 * hp + 1]
            o_ref[0, :, hp * LANES:(hp + 1) * LANES] = jnp.where(lane < MLA_V, even, odd).astype(o_ref.dtype)


def mla_flash(q, k, v, tq_pref=512, tk_pref=512):
    b, L, width = q.shape
    tq = _tile(L, tq_pref)
    tk = _tile(L, tk_pref)
    return pl.pallas_call(
        _mla_flash_kernel,
        grid=(b, L // tq, L // tk),
        in_specs=[pl.BlockSpec((1, tq, width), lambda bi, qi, ki: (bi, qi, 0)),
                  pl.BlockSpec((1, tk, width), lambda bi, qi, ki: (bi, ki, 0)),
                  pl.BlockSpec((1, tk, MLA_WIDTH), lambda bi, qi, ki: (bi, ki, 0))],
        out_specs=pl.BlockSpec((1, tq, MLA_WIDTH), lambda bi, qi, ki: (bi, qi, 0)),
        out_shape=jax.ShapeDtypeStruct((b, L, MLA_WIDTH), BF16),
        scratch_shapes=[pltpu.VMEM((MLA_HEADS, tq, 1), F32),
                        pltpu.VMEM((MLA_HEADS, tq, 1), F32),
                        pltpu.VMEM((MLA_HEADS, tq, LANES), F32)],
        compiler_params=_params("parallel", "parallel", "arbitrary"),
        name="mla_flash",
    )(q, k, v)


def _mla_layer_weights(cq_norm, ckv_norm, w_uq, w_ukv, q_norm, k_norm):
    pad = MLA_HEAD_PAD - MLA_QK
    wuq = jnp.pad(w_uq.reshape(MLA_Q_RANK, MLA_HEADS, MLA_QK), ((0, 0), (0, 0), (0, pad)))
    wukv = w_ukv.reshape(MLA_KV_RANK, MLA_HEADS, MLA_NOPE + MLA_V)
    wuk = jnp.pad(wukv[:, :, :MLA_NOPE], ((0, 0), (0, 0), (0, MLA_HEAD_PAD - MLA_NOPE)))
    wuv = wukv[:, :, MLA_NOPE:]
    return {
        "gcq": cq_norm.reshape(1, -1).astype(F32),
        "gckv": ckv_norm.reshape(1, -1).astype(F32),
        "wuq": wuq.reshape(MLA_Q_RANK, -1).astype(BF16),
        "wuk": wuk.reshape(MLA_KV_RANK, -1).astype(BF16),
        "wuv": wuv.reshape(MLA_KV_RANK, -1).astype(BF16),
        "gq": (jnp.pad(q_norm.astype(F32), (0, pad)) * (MLA_QK ** -0.5)).reshape(1, -1),
        "gk": jnp.pad(k_norm.astype(F32), (0, pad)).reshape(1, -1),
    }


RW_CHUNK = 64
RW_SUB = 16
RW_QUAD = 4
RW_QW = RW_QUAD * RW_HEAD_DIM
RW_SPLITS = (RW_WIDTH, 2 * RW_WIDTH, 3 * RW_WIDTH, 3 * RW_WIDTH + 2 * RW_DECAY_RANK,
             3 * RW_WIDTH + 2 * RW_DECAY_RANK + 2 * RW_A_RANK)
HALO = 8


def _split2(x):
    hi = x.astype(BF16)
    return hi, (x - hi.astype(F32)).astype(BF16)


def _dot3(x, w_hi, w_lo):
    hi, lo = _split2(x)
    return (jnp.dot(hi, w_hi, preferred_element_type=F32) + jnp.dot(lo, w_hi, preferred_element_type=F32)
            + jnp.dot(hi, w_lo, preferred_element_type=F32))


def _seg_sum(x, ones_bd):
    hi, lo = _split2(x)
    return jnp.dot(hi, ones_bd, preferred_element_type=F32) + jnp.dot(lo, ones_bd, preferred_element_type=F32)


def _rw_prep_kernel(p_ref, pp_ref, pn_ref, mu_ref, wup_hi, wup_lo, w0_ref, aup_hi, aup_lo, a0_ref,
                    gup_hi, gup_lo, kk_ref, ka_ref, ones_ref,
                    r_ref, v_ref, kkn_ref, g_ref, lw_ref, kd_ref, a_ref):
    j = pl.program_id(1)
    p = p_ref[0]
    tm = p.shape[0]
    prev_row = jnp.where(j > 0, pp_ref[0, HALO - 1:HALO, :], 0.0)
    next_row = jnp.where(j < pl.num_programs(1) - 1, pn_ref[0, 0:1, :], 0.0)
    row = lax.broadcasted_iota(jnp.int32, p.shape, 0)
    prev = jnp.where(row == 0, prev_row, pltpu.roll(p, 1, 0))
    nxt = jnp.where(row == tm - 1, next_row, pltpu.roll(p, tm - 1, 0))
    pm = p + mu_ref[...] * (0.5 * (prev + nxt) - p)
    s0, s1, s2, s3, s4 = RW_SPLITS
    r, k, v = pm[:, 0:s0], pm[:, s0:s1], pm[:, s1:s2]
    wd = jnp.tanh(pm[:, s2:s3])
    ad = pm[:, s3:s4]
    gd = pm[:, s4:]
    w_raw = w0_ref[...] + _dot3(wd, wup_hi[...], wup_lo[...])
    z = -w_raw
    softplus = jnp.maximum(z, 0.0) + jnp.log(1.0 + jnp.exp(-jnp.abs(z)))
    lw_ref[0] = -jnp.exp(-softplus - 0.5)
    a = jax.nn.sigmoid(a0_ref[...] + _dot3(ad, aup_hi[...], aup_lo[...]))
    a_ref[0] = a
    g_ref[0] = _dot3(jax.nn.sigmoid(gd), gup_hi[...], gup_lo[...])
    kk = k * kk_ref[...]
    kkn_ref[0] = kk * lax.rsqrt(_seg_sum(kk * kk, ones_ref[...]) + 1e-12)
    for d in range(2):
        sl = slice(d * RW_WIDTH, (d + 1) * RW_WIDTH)
        kd_ref[0, :, sl] = k * (1.0 + (a[:, sl] - 1.0) * ka_ref[...])
    r_ref[0] = r
    v_ref[0] = v


def rw_prep(p, lw, b, L):
    tm = _tile(L, 256)
    nh = L // HALO
    p3 = p.reshape(b, L, RW_IN)

    def full(a):
        return pl.BlockSpec(a.shape, lambda i, j: (0,) * a.ndim)

    weights = (lw["mu"], lw["wup_hi"], lw["wup_lo"], lw["w0"], lw["aup_hi"], lw["aup_lo"], lw["a0"],
               lw["gup_hi"], lw["gup_lo"], lw["k_k"], lw["k_a"], lw["ones_bd"])
    one = jax.ShapeDtypeStruct((b, L, RW_WIDTH), F32)
    two = jax.ShapeDtypeStruct((b, L, 2 * RW_WIDTH), F32)
    one_spec = pl.BlockSpec((1, tm, RW_WIDTH), lambda i, j: (i, j, 0))
    two_spec = pl.BlockSpec((1, tm, 2 * RW_WIDTH), lambda i, j: (i, j, 0))
    return pl.pallas_call(
        _rw_prep_kernel,
        grid=(b, L // tm),
        in_specs=[pl.BlockSpec((1, tm, RW_IN), lambda i, j: (i, j, 0)),
                  pl.BlockSpec((1, HALO, RW_IN), lambda i, j: (i, jnp.maximum(j * (tm // HALO) - 1, 0), 0)),
                  pl.BlockSpec((1, HALO, RW_IN), lambda i, j: (i, jnp.minimum((j + 1) * (tm // HALO), nh - 1), 0))]
        + [full(a) for a in weights],
        out_specs=[one_spec, one_spec, one_spec, one_spec, two_spec, two_spec, two_spec],
        out_shape=[one, one, one, one, two, two, two],
        compiler_params=_params("parallel", "parallel"),
        name="rw_prep",
    )(p3, p3, p3, *weights)


def _rw_chunk(r, lw, kd, v, kk, a, zt, bdm, bdm_b, reverse):
    C, NQ = RW_CHUNK, RW_QUAD
    ti = lax.broadcasted_iota(jnp.int32, (C, C), 0)
    si = lax.broadcasted_iota(jnp.int32, (C, C), 1)
    tri = jnp.where((si >= ti) if reverse else (si <= ti), 1.0, 0.0).astype(BF16)
    hi = lw.astype(BF16)
    r1 = lw - hi.astype(F32)
    mid = r1.astype(BF16)
    lo = (r1 - mid.astype(F32)).astype(BF16)
    cum = (jnp.dot(tri, hi, preferred_element_type=F32) + jnp.dot(tri, mid, preferred_element_type=F32)
           + jnp.dot(tri, lo, preferred_element_type=F32))
    tot = cum[0:1] if reverse else cum[C - 1:C]
    pinv = jnp.exp(-cum)
    pend = jnp.exp(tot - cum)
    kka = kk * a
    kap = (kk * jnp.exp(cum - lw)).astype(BF16)
    bet = (kka * pinv).astype(BF16)
    kt = (kd * pinv).astype(BF16)
    rt = (r * jnp.exp(cum)).astype(BF16)

    def bd(x):
        return jnp.concatenate([x] * NQ, axis=0) * bdm_b

    def mm(m, x):
        return jnp.dot(m.astype(BF16), bd(x.astype(BF16)), preferred_element_type=F32)

    wt = lax.broadcasted_iota(jnp.int32, (C, NQ * C), 0)
    ws = lax.broadcasted_iota(jnp.int32, (C, NQ * C), 1) & (C - 1)
    strict = (ws > wt) if reverse else (ws < wt)
    incl = (ws >= wt) if reverse else (ws <= wt)
    same = (ws // RW_SUB) == (wt // RW_SUB)

    x2 = jnp.concatenate([kap, rt], axis=0)
    y2 = jnp.concatenate([bd(bet), bd(kt)], axis=0)
    aw = lax.dot_general(x2, y2, (((1,), (1,)), ((), ())), preferred_element_type=F32)
    a_ab = jnp.where(strict, aw[:C, :NQ * C], 0.0)
    a_ak = jnp.where(strict, aw[:C, NQ * C:], 0.0)
    a_rb = jnp.where(incl, aw[C:, :NQ * C], 0.0)
    a_rk = jnp.where(incl, aw[C:, NQ * C:], 0.0)
    zz = lax.dot_general(x2, zt.astype(BF16), (((1,), (1,)), ((), ())), preferred_element_type=F32)
    rhs = -(zz[:C] + mm(a_ak, v))
    d = jnp.where(same, a_ab, 0.0)
    e = a_ab - d
    d2 = mm(d, d)
    d4 = mm(d2, d2)
    d8 = mm(d4, d4)
    t = jnp.where(ws == wt, 1.0, 0.0) - d
    t = t + mm(t, d2)
    t = t + mm(t, d4)
    t = t + mm(t, d8)
    n = mm(t, e)
    n2 = mm(n, n)
    u = mm(t, rhs)
    u = u + mm(n2, u)
    u = u - mm(n, u)
    y = zz[C:] + mm(a_rk, v) + mm(a_rb, u)
    lhs = jnp.concatenate([v, u], axis=0).astype(BF16)
    rhs2 = jnp.concatenate([kd * pend, kka * pend], axis=0).astype(BF16)
    upd = lax.dot_general(lhs, rhs2, (((0,), (0,)), ((), ())), preferred_element_type=F32)
    return y, zt * jnp.exp(tot) + upd * bdm


def _rw_scan_kernel(rf, vf, kkf, lwf, kdf, af, rb, vb, kkb, lwb, kdb, ab, bdm_ref, yf_ref, yb_ref, zt_ref):
    @pl.when(pl.program_id(1) == 0)
    def _():
        zt_ref[...] = jnp.zeros(zt_ref.shape, F32)

    bdm = bdm_ref[...]
    bdm_b = bdm.astype(BF16)
    chains = ((False, (rf, lwf, kdf, vf, kkf, af), yf_ref), (True, (rb, lwb, kdb, vb, kkb, ab), yb_ref))
    for di, (reverse, refs, y_ref) in enumerate(chains):
        for q in range(RW_HEADS // RW_QUAD):
            sl = slice(q * RW_QW, (q + 1) * RW_QW)
            args = [ref[0, :, sl] for ref in refs]
            idx = di * (RW_HEADS // RW_QUAD) + q
            y, zt = _rw_chunk(*args, zt_ref[idx], bdm, bdm_b, reverse)
            y_ref[0, :, sl] = y
            zt_ref[idx] = zt


def rw_scan(r, v, kk, lw, kd, a, bdm):
    b, L, _ = r.shape
    nc = L // RW_CHUNK
    blk = (1, RW_CHUNK, RW_WIDTH)
    f1 = pl.BlockSpec(blk, lambda bi, c: (bi, c, 0))
    b1 = pl.BlockSpec(blk, lambda bi, c: (bi, nc - 1 - c, 0))
    b2 = pl.BlockSpec(blk, lambda bi, c: (bi, nc - 1 - c, 1))
    out = jax.ShapeDtypeStruct((b, L, RW_WIDTH), F32)
    return pl.pallas_call(
        _rw_scan_kernel,
        grid=(b, nc),
        in_specs=[f1, f1, f1, f1, f1, f1, b1, b1, b1, b2, b2, b2,
                  pl.BlockSpec(bdm.shape, lambda bi, c: (0, 0))],
        out_specs=[f1, b1],
        out_shape=[out, out],
        scratch_shapes=[pltpu.VMEM((2 * RW_HEADS // RW_QUAD, RW_QW, RW_QW), F32)],
        compiler_params=_params("parallel", "arbitrary"),
        name="rw_scan",
    )(r, v, kk, lw, kd, a, r, v, kk, lw, kd, a, bdm)


def _rw_post_kernel(yf_ref, yb_ref, r_ref, v_ref, kd_ref, g_ref, lnw_ref, lnb_ref, rk_ref, ones_ref, o_ref):
    ones_bd = ones_ref[...]
    inv_n = 1.0 / RW_HEAD_DIM
    y = yf_ref[...] + yb_ref[...]
    yc = y - _seg_sum(y, ones_bd) * inv_n
    var = _seg_sum(yc * yc, ones_bd) * inv_n
    y = yc * lax.rsqrt(var + RW_LN_EPS) * lnw_ref[...] + lnb_ref[...]
    kd = kd_ref[:, 0:RW_WIDTH] + kd_ref[:, RW_WIDTH:]
    bonus = _seg_sum(r_ref[...] * kd * rk_ref[...], ones_bd)
    o_ref[...] = ((y + bonus * v_ref[...]) * g_ref[...]).astype(o_ref.dtype)


def rw_post(yf, yb, r, v, kd, g, lw):
    t = yf.shape[0]
    tm = _tile(t, 512)
    one = pl.BlockSpec((tm, RW_WIDTH), lambda i: (i, 0))
    two = pl.BlockSpec((tm, 2 * RW_WIDTH), lambda i: (i, 0))

    def full(a):
        return pl.BlockSpec(a.shape, lambda i: (0,) * a.ndim)

    weights = (lw["ln_w"], lw["ln_b"], lw["r_k"], lw["ones_bd"])
    return pl.pallas_call(
        _rw_post_kernel,
        grid=(t // tm,),
        in_specs=[one, one, one, one, two, one] + [full(a) for a in weights],
        out_specs=one,
        out_shape=jax.ShapeDtypeStruct((t, RW_WIDTH), BF16),
        compiler_params=_params("parallel"),
        name="rw_post",
    )(yf, yb, r, v, kd, g, *weights)


def _block_diag2(w):
    z = jnp.zeros_like(w[0])
    return jnp.concatenate([jnp.concatenate([w[0], z], axis=1), jnp.concatenate([z, w[1]], axis=1)], axis=0)


def _hi_lo(w):
    w = w.astype(F32)
    hi = w.astype(BF16)
    return hi, (w - hi.astype(F32)).astype(BF16)


def _rw_layer_weights(mu, w0, w_up, a0, a_up, g_up, k_k, k_a, r_k, ln_w, ln_b):
    wup_hi, wup_lo = _hi_lo(_block_diag2(w_up))
    aup_hi, aup_lo = _hi_lo(_block_diag2(a_up))
    gup_hi, gup_lo = _hi_lo(g_up)
    head = np.arange(RW_WIDTH) // RW_HEAD_DIM

    def row(t):
        return t.reshape(1, -1).astype(F32)

    return {
        "mu": row(mu), "wup_hi": wup_hi, "wup_lo": wup_lo, "w0": row(w0),
        "aup_hi": aup_hi, "aup_lo": aup_lo, "a0": row(a0), "gup_hi": gup_hi, "gup_lo": gup_lo,
        "k_k": row(k_k), "k_a": row(k_a), "r_k": row(r_k), "ln_w": row(ln_w), "ln_b": row(ln_b),
        "ones_bd": jnp.asarray(head[:, None] == head[None, :], BF16),
    }


def rwkv7_mix(p, lw, bdm, b, L):
    r, v, kk, g, lwd, kd, a = rw_prep(p, lw, b, L)
    yf, yb = rw_scan(r, v, kk, lwd, kd, a, bdm)
    t = b * L

    def flat(x):
        return x.reshape(t, x.shape[-1])

    return rw_post(flat(yf), flat(yb), flat(r), flat(v), flat(kd), flat(g), lw)


def _merge_kernel(x_ref, ya_ref, yb_ref, yc_ref, gin_ref, bg_ref, pa_ref, pb_ref, pc_ref, wo_ref, o_ref):
    mixed = None
    for i, (y_ref, p_ref) in enumerate(((ya_ref, pa_ref), (yb_ref, pb_ref), (yc_ref, pc_ref))):
        sl = slice(i * D_MODEL, (i + 1) * D_MODEL)
        gate = jax.nn.sigmoid(gin_ref[:, sl].astype(F32) + bg_ref[:, sl])
        term = gate * jnp.dot(y_ref[...], p_ref[...], preferred_element_type=F32)
        mixed = term if mixed is None else mixed + term
    o_ref[...] = x_ref[...] + jnp.dot(mixed.astype(BF16), wo_ref[...], preferred_element_type=F32)


def merge(x, ya, yb, yc, gate_in, b_gate, pa, pb, pc, wo):
    t = x.shape[0]
    tm = _tile(t, 512)

    def rows(w):
        return pl.BlockSpec((tm, w), lambda i: (i, 0))

    def full(a):
        return pl.BlockSpec(a.shape, lambda i: (0,) * a.ndim)

    return pl.pallas_call(
        _merge_kernel,
        grid=(t // tm,),
        in_specs=[rows(D_MODEL), rows(NA_WIDTH), rows(MLA_WIDTH), rows(RW_WIDTH), rows(N_BRANCH * D_MODEL),
                  full(b_gate), full(pa), full(pb), full(pc), full(wo)],
        out_specs=rows(D_MODEL),
        out_shape=jax.ShapeDtypeStruct((t, D_MODEL), F32),
        compiler_params=_params("parallel"),
        name="merge",
    )(x, ya, yb, yc, gate_in, b_gate, pa, pb, pc, wo)


def _ffn_kernel(x_ref, g_ref, wg_ref, wu_ref, wd_ref, o_ref, h_ref, acc_ref):
    j = pl.program_id(1)

    @pl.when(j == 0)
    def _():
        x = x_ref[...]
        ms = jnp.mean(x * x, axis=-1, keepdims=True)
        h_ref[...] = (x * lax.rsqrt(ms + NORM_EPS) * g_ref[...]).astype(BF16)
        acc_ref[...] = x

    h = h_ref[...]
    gate = jnp.dot(h, wg_ref[...], preferred_element_type=F32)
    up = jnp.dot(h, wu_ref[...], preferred_element_type=F32)
    act = (gate * jax.nn.sigmoid(gate) * up).astype(BF16)
    acc_ref[...] += jnp.dot(act, wd_ref[...], preferred_element_type=F32)

    @pl.when(j == pl.num_programs(1) - 1)
    def _():
        o_ref[...] = acc_ref[...]


def ffn(x, g, wg, wu, wd, tm_pref=1024, tf_pref=256):
    t, d = x.shape
    f = wg.shape[1]
    tm = _tile(t, tm_pref)
    tf = _tile(f, tf_pref, LANES)
    return pl.pallas_call(
        _ffn_kernel,
        grid=(t // tm, f // tf),
        in_specs=[pl.BlockSpec((tm, d), lambda i, j: (i, 0)),
                  pl.BlockSpec((1, d), lambda i, j: (0, 0)),
                  pl.BlockSpec((d, tf), lambda i, j: (0, j)),
                  pl.BlockSpec((d, tf), lambda i, j: (0, j)),
                  pl.BlockSpec((tf, d), lambda i, j: (j, 0))],
        out_specs=pl.BlockSpec((tm, d), lambda i, j: (i, 0)),
        out_shape=jax.ShapeDtypeStruct((t, d), F32),
        scratch_shapes=[pltpu.VMEM((tm, d), BF16), pltpu.VMEM((tm, d), F32)],
        compiler_params=_params("parallel", "arbitrary"),
        name="ffn",
    )(x, g.reshape(1, d), wg, wu, wd)


IN_SIZES = (NA_WIDTH, NA_WIDTH, NA_WIDTH, MLA_Q_RANK, MLA_KV_RANK, MLA_ROPE, RW_IN, N_BRANCH * D_MODEL)
IN_SPLITS = tuple(int(s) for s in np.cumsum(IN_SIZES)[:-1])


def _layer_weights(l, norm1_g, w_in, b_gate, na_q_norm, na_k_norm, na_rpb, na_proj,
                   mla_cq_norm, mla_ckv_norm, mla_w_uq, mla_w_ukv, mla_q_norm, mla_k_norm, mla_proj,
                   rw_mu, rw_w0, rw_w_up, rw_a0, rw_a_up, rw_g_up, rw_k_k, rw_k_a, rw_r_k, rw_ln_w, rw_ln_b, rw_proj,
                   w_out, norm2_g, ffn_w_gate, ffn_w_up, ffn_w_down):
    w = w_in[l]
    kr_lo = MLA_NOPE
    w_kr = jnp.pad(w[:, IN_SPLITS[4]:IN_SPLITS[5]], ((0, 0), (kr_lo, MLA_HEAD_PAD - kr_lo - MLA_ROPE)))
    w_attn = jnp.concatenate([w[:, :IN_SPLITS[4]], w_kr], axis=1).astype(BF16)
    return {
        "norm1_g": norm1_g[l], "w_attn": w_attn,
        "w_rw": w[:, IN_SPLITS[5]:IN_SPLITS[6]].astype(BF16),
        "w_gate": w[:, IN_SPLITS[6]:].astype(BF16),
        "b_gate": b_gate[l].reshape(1, -1).astype(F32),
        "na_gq": na_q_norm[l].astype(F32) * (NA_HEAD_DIM ** -0.5), "na_gk": na_k_norm[l].astype(F32),
        "na_bias": _na_bias_table(na_rpb[l]),
        "na_proj": na_proj[l].astype(BF16), "mla_proj": mla_proj[l].astype(BF16), "rw_proj": rw_proj[l].astype(BF16),
        "mla": _mla_layer_weights(mla_cq_norm[l], mla_ckv_norm[l], mla_w_uq[l], mla_w_ukv[l], mla_q_norm[l], mla_k_norm[l]),
        "rw": _rw_layer_weights(rw_mu[l], rw_w0[l], rw_w_up[l], rw_a0[l], rw_a_up[l], rw_g_up[l],
                                rw_k_k[l], rw_k_a[l], rw_r_k[l], rw_ln_w[l], rw_ln_b[l]),
        "w_out": w_out[l].astype(BF16), "norm2_g": norm2_g[l],
        "ffn_w_gate": ffn_w_gate[l].astype(BF16), "ffn_w_up": ffn_w_up[l].astype(BF16),
        "ffn_w_down": ffn_w_down[l].astype(BF16),
    }


def _trunk(x, layers):
    b, L, d = x.shape
    t = b * L
    x = x.reshape(t, d)
    tabs = _rope_tables(L)
    head = np.arange(RW_QW) // RW_HEAD_DIM
    bdm = jnp.asarray(head[:, None] == head[None, :], F32)
    for lw in layers:
        attn_in = norm_matmul(x, lw["norm1_g"], lw["w_attn"], BF16)
        rw_in = norm_matmul(x, lw["norm1_g"], lw["w_rw"], F32)
        gate_in = norm_matmul(x, lw["norm1_g"], lw["w_gate"], BF16)
        qa, ka, va = na_prep(attn_in, lw["na_gq"], lw["na_gk"], b, L)
        y_a = na_attention(qa, ka, va, lw["na_bias"]).reshape(t, NA_WIDTH)
        qm, km, vm = mla_prep(attn_in, lw["mla"], tabs, b, L)
        y_b = mla_flash(qm, km, vm).reshape(t, MLA_WIDTH)
        y_c = rwkv7_mix(rw_in, lw["rw"], bdm, b, L)
        x = merge(x, y_a, y_b, y_c, gate_in, lw["b_gate"], lw["na_proj"], lw["mla_proj"], lw["rw_proj"], lw["w_out"])
        x = ffn(x, lw["norm2_g"], lw["ffn_w_gate"], lw["ffn_w_up"], lw["ffn_w_down"])
    return x.reshape(b, L, d)


def kernel(x_prompt, x_sample, norm1_g, w_in, b_gate, na_q_norm, na_k_norm, na_rpb, na_proj, mla_cq_norm, mla_ckv_norm, mla_w_uq, mla_w_ukv, mla_q_norm, mla_k_norm, mla_proj, rw_mu, rw_w0, rw_w_up, rw_a0, rw_a_up, rw_g_up, rw_k_k, rw_k_a, rw_r_k, rw_ln_w, rw_ln_b, rw_proj, w_out, norm2_g, ffn_w_gate, ffn_w_up, ffn_w_down):
    weights = (norm1_g, w_in, b_gate, na_q_norm, na_k_norm, na_rpb, na_proj,
               mla_cq_norm, mla_ckv_norm, mla_w_uq, mla_w_ukv, mla_q_norm, mla_k_norm, mla_proj,
               rw_mu, rw_w0, rw_w_up, rw_a0, rw_a_up, rw_g_up, rw_k_k, rw_k_a, rw_r_k, rw_ln_w, rw_ln_b, rw_proj,
               w_out, norm2_g, ffn_w_gate, ffn_w_up, ffn_w_down)
    layers = [_layer_weights(l, *weights) for l in range(norm1_g.shape[0])]
    return (_trunk(x_prompt, layers), _trunk(x_sample, layers))
```

```python
import functools

import jax
import jax.numpy as jnp
import numpy as np
from jax import lax
from jax.experimental import pallas as pl
from jax.experimental.pallas import tpu as pltpu

F32 = jnp.float32
BF16 = jnp.bfloat16

D_MODEL = 1024
GRID_W = 64
N_BRANCH = 3
NORM_EPS = 1e-6

NA_HEADS = 8
NA_HEAD_DIM = 64
NA_WIDTH = NA_HEADS * NA_HEAD_DIM
NA_WIN_ROWS = 8
NA_WIN_COLS = 16
NA_RPB_ROWS = 2 * NA_WIN_ROWS - 1
NA_RPB_COLS = 2 * NA_WIN_COLS - 1
NA_MASK = -1e30

MLA_HEADS = 8
MLA_NOPE = 64
MLA_ROPE = 32
MLA_QK = MLA_NOPE + MLA_ROPE
MLA_V = 64
MLA_WIDTH = MLA_HEADS * MLA_V
MLA_Q_RANK = 256
MLA_KV_RANK = 128
MLA_HEAD_PAD = 128
ROPE_THETA = 10000.0

RW_HEADS = 8
RW_HEAD_DIM = 64
RW_WIDTH = RW_HEADS * RW_HEAD_DIM
RW_DECAY_RANK = 64
RW_A_RANK = 64
RW_G_RANK = 128
RW_LN_EPS = 64e-5
RW_IN = 3 * RW_WIDTH + 2 * RW_DECAY_RANK + 2 * RW_A_RANK + RW_G_RANK

D_FF = 2816

VMEM_LIMIT_BYTES = 56 * 1024 * 1024
LANES = 128


def _params(*sem):
    return pltpu.CompilerParams(dimension_semantics=sem, vmem_limit_bytes=VMEM_LIMIT_BYTES)


def _tile(n, pref, mult=8):
    if n <= pref:
        return n
    t = (pref // mult) * mult
    while t >= mult:
        if n % t == 0:
            return t
        t -= mult
    return n


def _norm_matmul_kernel(x_ref, g_ref, w_ref, o_ref, h_ref):
    @pl.when(pl.program_id(1) == 0)
    def _():
        x = x_ref[...]
        ms = jnp.mean(x * x, axis=-1, keepdims=True)
        h_ref[...] = (x * lax.rsqrt(ms + NORM_EPS) * g_ref[...]).astype(BF16)

    o_ref[...] = jnp.dot(h_ref[...], w_ref[...], preferred_element_type=F32).astype(o_ref.dtype)


def norm_matmul(x, g, w, out_dtype, tm_pref=1024, tn_pref=1024):
    t, d = x.shape
    n = w.shape[1]
    tm = _tile(t, tm_pref)
    tn = _tile(n, tn_pref, LANES)
    return pl.pallas_call(
        _norm_matmul_kernel,
        grid=(t // tm, n // tn),
        in_specs=[
            pl.BlockSpec((tm, d), lambda i, j: (i, 0)),
            pl.BlockSpec((1, d), lambda i, j: (0, 0)),
            pl.BlockSpec((d, tn), lambda i, j: (0, j)),
        ],
        out_specs=pl.BlockSpec((tm, tn), lambda i, j: (i, j)),
        out_shape=jax.ShapeDtypeStruct((t, n), out_dtype),
        scratch_shapes=[pltpu.VMEM((tm, d), BF16)],
        compiler_params=_params("parallel", "arbitrary"),
        name="norm_matmul",
    )(x, g.reshape(1, d), w)


def _na_prep_kernel(x_ref, gq_ref, gk_ref, q_ref, k_ref, v_ref):
    for h in range(NA_HEADS):
        lo = h * NA_HEAD_DIM
        q = x_ref[0, :, lo:lo + NA_HEAD_DIM].astype(F32)
        k = x_ref[0, :, NA_WIDTH + lo:NA_WIDTH + lo + NA_HEAD_DIM].astype(F32)
        q = q * lax.rsqrt(jnp.mean(q * q, axis=-1, keepdims=True) + NORM_EPS) * gq_ref[...]
        k = k * lax.rsqrt(jnp.mean(k * k, axis=-1, keepdims=True) + NORM_EPS) * gk_ref[...]
        q_ref[0, h] = q.astype(BF16)
        k_ref[0, h] = k.astype(BF16)
        v_ref[0, h] = x_ref[0, :, 2 * NA_WIDTH + lo:2 * NA_WIDTH + lo + NA_HEAD_DIM]


def na_prep(attn_in, gq, gk, b, L):
    tm = _tile(L, 512)
    x3 = attn_in.reshape(b, L, attn_in.shape[-1])
    hm = jax.ShapeDtypeStruct((b, NA_HEADS, L, NA_HEAD_DIM), BF16)
    hm_spec = pl.BlockSpec((1, NA_HEADS, tm, NA_HEAD_DIM), lambda i, j: (i, 0, j, 0))
    return pl.pallas_call(
        _na_prep_kernel,
        grid=(b, L // tm),
        in_specs=[
            pl.BlockSpec((1, tm, 3 * NA_WIDTH), lambda i, j: (i, j, 0)),
            pl.BlockSpec((1, NA_HEAD_DIM), lambda i, j: (0, 0)),
            pl.BlockSpec((1, NA_HEAD_DIM), lambda i, j: (0, 0)),
        ],
        out_specs=[hm_spec, hm_spec, hm_spec],
        out_shape=[hm, hm, hm],
        compiler_params=_params("parallel", "parallel"),
        name="na_prep",
    )(x3, gq.reshape(1, -1), gk.reshape(1, -1))


NA_ROWS_PER_STEP = 8
NA_BLOCK = NA_ROWS_PER_STEP * GRID_W
NA_KEYS = NA_WIN_ROWS * GRID_W


def _na_attn_kernel(q_ref, kp_ref, kc_ref, kn_ref, vp_ref, vc_ref, vn_ref, bias_ref, o_ref, kbuf, vbuf, *, rows):
    i = pl.program_id(1)
    kbuf[:, 0:NA_BLOCK] = kp_ref[0]
    kbuf[:, NA_BLOCK:2 * NA_BLOCK] = kc_ref[0]
    kbuf[:, 2 * NA_BLOCK:3 * NA_BLOCK] = kn_ref[0]
    vbuf[:, 0:NA_BLOCK] = vp_ref[0]
    vbuf[:, NA_BLOCK:2 * NA_BLOCK] = vc_ref[0]
    vbuf[:, 2 * NA_BLOCK:3 * NA_BLOCK] = vn_ref[0]

    def row_body(j, carry):
        r = i * NA_ROWS_PER_STEP + j
        rs = jnp.clip(r - NA_WIN_ROWS // 2, 0, rows - NA_WIN_ROWS)
        off = pl.multiple_of((rs - (i - 1) * NA_ROWS_PER_STEP) * GRID_W, GRID_W)
        pat = r - rs
        qoff = pl.multiple_of(j * GRID_W, GRID_W)
        heads = range(NA_HEADS)
        s = [lax.dot_general(q_ref[0, h, pl.ds(qoff, GRID_W), :], kbuf[h, pl.ds(off, NA_KEYS), :],
                             (((1,), (1,)), ((), ())), preferred_element_type=F32) + bias_ref[pat, h]
             for h in heads]
        p = [jnp.exp(x - jnp.max(x, axis=-1, keepdims=True)) for x in s]
        l = [jnp.sum(x, axis=-1, keepdims=True) for x in p]
        o = [jnp.dot(p[h].astype(BF16), vbuf[h, pl.ds(off, NA_KEYS), :], preferred_element_type=F32) / l[h]
             for h in heads]
        for h in heads:
            o_ref[0, pl.ds(qoff, GRID_W), h * NA_HEAD_DIM:(h + 1) * NA_HEAD_DIM] = o[h].astype(o_ref.dtype)
        return carry

    lax.fori_loop(0, NA_ROWS_PER_STEP, row_body, 0)


def _na_bias_table(rpb):
    pat = np.arange(NA_WIN_ROWS)[:, None]
    w = np.arange(NA_WIN_ROWS)[None, :]
    dr_idx = w - pat + NA_WIN_ROWS - 1
    qc = np.arange(GRID_W)[:, None]
    kc = np.arange(GRID_W)[None, :]
    win_start = np.clip(qc - NA_WIN_COLS // 2, 0, GRID_W - NA_WIN_COLS)
    ok = (kc >= win_start) & (kc < win_start + NA_WIN_COLS)
    dc_idx = np.clip(kc - qc + NA_WIN_COLS - 1, 0, NA_RPB_COLS - 1)
    t = rpb.astype(F32)[:, dr_idx][:, :, :, dc_idx]
    t = jnp.where(jnp.asarray(ok)[None, None, None], t, NA_MASK)
    t = jnp.transpose(t, (1, 0, 3, 2, 4))
    return t.reshape(NA_WIN_ROWS, NA_HEADS, GRID_W, NA_KEYS)


def na_attention(q, k, v, bias):
    b, _, L, _ = q.shape
    rows = L // GRID_W
    assert rows % NA_ROWS_PER_STEP == 0 and rows >= NA_WIN_ROWS
    nblk = rows // NA_ROWS_PER_STEP
    blk = (1, NA_HEADS, NA_BLOCK, NA_HEAD_DIM)
    cur = pl.BlockSpec(blk, lambda bi, i: (bi, 0, i, 0))
    prev = pl.BlockSpec(blk, lambda bi, i: (bi, 0, jnp.maximum(i - 1, 0), 0))
    nxt = pl.BlockSpec(blk, lambda bi, i: (bi, 0, jnp.minimum(i + 1, nblk - 1), 0))
    buf = pltpu.VMEM((NA_HEADS, 3 * NA_BLOCK, NA_HEAD_DIM), BF16)
    return pl.pallas_call(
        functools.partial(_na_attn_kernel, rows=rows),
        grid=(b, nblk),
        in_specs=[cur, prev, cur, nxt, prev, cur, nxt,
                  pl.BlockSpec(bias.shape, lambda bi, i: (0, 0, 0, 0))],
        out_specs=pl.BlockSpec((1, NA_BLOCK, NA_WIDTH), lambda bi, i: (bi, i, 0)),
        out_shape=jax.ShapeDtypeStruct((b, L, NA_WIDTH), BF16),
        scratch_shapes=[buf, buf],
        compiler_params=_params("parallel", "parallel"),
        name="na_attn",
    )(q, k, k, k, v, v, v, bias)


def _rope_tables(L):
    t = np.arange(L)
    row = (t // GRID_W).astype(np.float32)
    col = (t % GRID_W).astype(np.float32)
    n_freq = MLA_ROPE // 4
    inv_freq = jnp.asarray(ROPE_THETA, F32) ** (-jnp.arange(n_freq, dtype=F32) / n_freq)
    ang = jnp.concatenate([jnp.asarray(row)[:, None] * inv_freq, jnp.asarray(col)[:, None] * inv_freq], axis=-1)
    cos, sin = jnp.cos(ang), jnp.sin(ang)
    half = MLA_ROPE // 2
    ones = jnp.ones((L, MLA_NOPE), F32)
    zpad = jnp.zeros((L, MLA_HEAD_PAD - MLA_QK), F32)
    znope = jnp.zeros((L, MLA_NOPE), F32)
    zhalf = jnp.zeros((L, half), F32)
    c_tab = jnp.concatenate([ones, cos, cos, zpad], axis=-1)
    s1_tab = jnp.concatenate([znope, -sin, zhalf, zpad], axis=-1)
    s2_tab = jnp.concatenate([znope, zhalf, sin, zpad], axis=-1)
    return c_tab, s1_tab, s2_tab


def _mla_prep_kernel(x_ref, gcq_ref, gckv_ref, wuq_ref, wuk_ref, wuv_ref, vones_ref, gq_ref, gk_ref,
                     c_ref, s1_ref, s2_ref, q_ref, k_ref, v_ref):
    half = MLA_ROPE // 2
    cq = x_ref[0, :, 0:MLA_Q_RANK].astype(F32)
    ckv = x_ref[0, :, MLA_Q_RANK:MLA_Q_RANK + MLA_KV_RANK].astype(F32)
    kr = x_ref[0, :, MLA_Q_RANK + MLA_KV_RANK:].astype(F32)
    cqn = (cq * lax.rsqrt(jnp.mean(cq * cq, axis=-1, keepdims=True) + NORM_EPS) * gcq_ref[...]).astype(BF16)
    ckvn = (ckv * lax.rsqrt(jnp.mean(ckv * ckv, axis=-1, keepdims=True) + NORM_EPS) * gckv_ref[...]).astype(BF16)
    q_raw = jnp.dot(cqn, wuq_ref[...], preferred_element_type=F32)
    k_raw = jnp.dot(ckvn, wuk_ref[...], preferred_element_type=F32)
    v_ref[0] = (jnp.dot(ckvn, wuv_ref[...], preferred_element_type=F32) + vones_ref[...]).astype(BF16)
    c, s1, s2 = c_ref[...], s1_ref[...], s2_ref[...]

    def norm_rope(t, g):
        t = t * lax.rsqrt(jnp.sum(t * t, axis=-1, keepdims=True) * (1.0 / MLA_QK) + NORM_EPS) * g
        return t * c + pltpu.roll(t, MLA_HEAD_PAD - half, 1) * s1 + pltpu.roll(t, half, 1) * s2

    for h in range(MLA_HEADS):
        sl = slice(h * MLA_HEAD_PAD, (h + 1) * MLA_HEAD_PAD)
        q_ref[0, :, sl] = norm_rope(q_raw[:, sl], gq_ref[...]).astype(BF16)
        k_ref[0, :, sl] = norm_rope(k_raw[:, sl] + kr, gk_ref[...]).astype(BF16)


def mla_prep(attn_in, lw, tabs, b, L):
    tm = _tile(L, 512)
    width = MLA_HEADS * MLA_HEAD_PAD
    x3 = attn_in.reshape(b, L, attn_in.shape[-1])
    in_w = MLA_Q_RANK + MLA_KV_RANK + MLA_HEAD_PAD
    col_blk = (3 * NA_WIDTH) // in_w
    assert col_blk * in_w == 3 * NA_WIDTH

    def full(a):
        return pl.BlockSpec(a.shape, lambda i, j: (0,) * a.ndim)

    tab_spec = pl.BlockSpec((tm, MLA_HEAD_PAD), lambda i, j: (j, 0))
    weights = (lw["gcq"], lw["gckv"], lw["wuq"], lw["wuk"], lw["wuv"], lw["v_ones"], lw["gq"], lw["gk"])
    slab_spec = pl.BlockSpec((1, tm, width), lambda i, j: (i, j, 0))
    slab = jax.ShapeDtypeStruct((b, L, width), BF16)
    return pl.pallas_call(
        _mla_prep_kernel,
        grid=(b, L // tm),
        in_specs=[pl.BlockSpec((1, tm, in_w), lambda i, j: (i, j, col_blk))]
        + [full(a) for a in weights] + [tab_spec] * 3,
        out_specs=[slab_spec, slab_spec, slab_spec],
        out_shape=[slab, slab, slab],
        compiler_params=_params("parallel", "parallel"),
        name="mla_prep",
    )(x3, *weights, *tabs)


def _mla_flash_kernel(q_ref, k_ref, v_ref, o_ref, m_ref, acc_ref):
    ki = pl.program_id(2)
    tk = k_ref.shape[1]

    @pl.when(ki == 0)
    def _():
        m_ref[...] = jnp.full(m_ref.shape, -jnp.inf, F32)
        acc_ref[...] = jnp.zeros(acc_ref.shape, F32)

    for h in range(MLA_HEADS):
        sl = slice(h * MLA_HEAD_PAD, (h + 1) * MLA_HEAD_PAD)
        s = lax.dot_general(q_ref[0, :, sl], k_ref[0, :, sl], (((1,), (1,)), ((), ())), preferred_element_type=F32)
        m_prev = m_ref[h]
        m_next = jnp.maximum(m_prev, jnp.max(s, axis=-1, keepdims=True))
        alpha = jnp.exp2(m_prev - m_next)
        p = jnp.exp2(s - jnp.concatenate([m_next] * (tk // LANES), axis=1))
        acc_ref[h] = alpha * acc_ref[h] + jnp.dot(p.astype(BF16), v_ref[0, :, sl], preferred_element_type=F32)
        m_ref[h] = m_next

    @pl.when(ki == pl.num_programs(2) - 1)
    def _():
        lane = lax.broadcasted_iota(jnp.int32, (acc_ref.shape[1], LANES), 1)

        def normalised(h):
            acc = acc_ref[h]
            return acc / pltpu.roll(acc, MLA_V, 1)

        for hp in range(MLA_HEADS // 2):
            pair = jnp.where(lane < MLA_V, normalised(2 * hp), pltpu.roll(normalised(2 * hp + 1), MLA_V, 1))
            o_ref[0, :, hp * LANES:(hp + 1) * LANES] = pair.astype(o_ref.dtype)


def mla_flash(q, k, v, tq_pref=512, tk_pref=2048):
    b, L, width = q.shape
    tq = _tile(L, tq_pref)
    tk = _tile(L, tk_pref, LANES)
    return pl.pallas_call(
        _mla_flash_kernel,
        grid=(b, L // tq, L // tk),
        in_specs=[pl.BlockSpec((1, tq, width), lambda bi, qi, ki: (bi, qi, 0)),
                  pl.BlockSpec((1, tk, width), lambda bi, qi, ki: (bi, ki, 0)),
                  pl.BlockSpec((1, tk, width), lambda bi, qi, ki: (bi, ki, 0))],
        out_specs=pl.BlockSpec((1, tq, MLA_WIDTH), lambda bi, qi, ki: (bi, qi, 0)),
        out_shape=jax.ShapeDtypeStruct((b, L, MLA_WIDTH), BF16),
        scratch_shapes=[pltpu.VMEM((MLA_HEADS, tq, LANES), F32),
                        pltpu.VMEM((MLA_HEADS, tq, LANES), F32)],
        compiler_params=_params("parallel", "parallel", "arbitrary"),
        name="mla_flash",
    )(q, k, v)


def _mla_layer_weights(cq_norm, ckv_norm, w_uq, w_ukv, q_norm, k_norm):
    pad = MLA_HEAD_PAD - MLA_QK
    wuq = jnp.pad(w_uq.reshape(MLA_Q_RANK, MLA_HEADS, MLA_QK), ((0, 0), (0, 0), (0, pad)))
    wukv = w_ukv.reshape(MLA_KV_RANK, MLA_HEADS, MLA_NOPE + MLA_V)
    wuk = jnp.pad(wukv[:, :, :MLA_NOPE], ((0, 0), (0, 0), (0, MLA_HEAD_PAD - MLA_NOPE)))
    wuv = jnp.pad(wukv[:, :, MLA_NOPE:], ((0, 0), (0, 0), (0, MLA_HEAD_PAD - MLA_V)))
    v_ones = np.tile(np.arange(MLA_HEAD_PAD) >= MLA_V, MLA_HEADS).astype(np.float32)
    return {
        "v_ones": jnp.asarray(v_ones).reshape(1, -1),
        "gcq": cq_norm.reshape(1, -1).astype(F32),
        "gckv": ckv_norm.reshape(1, -1).astype(F32),
        "wuq": wuq.reshape(MLA_Q_RANK, -1).astype(BF16),
        "wuk": wuk.reshape(MLA_KV_RANK, -1).astype(BF16),
        "wuv": wuv.reshape(MLA_KV_RANK, -1).astype(BF16),
        "gq": (jnp.pad(q_norm.astype(F32), (0, pad)) * (MLA_QK ** -0.5 * np.log2(np.e))).reshape(1, -1),
        "gk": jnp.pad(k_norm.astype(F32), (0, pad)).reshape(1, -1),
    }


RW_CHUNK = 64
RW_SUB = 16
RW_QUAD = 4
RW_QW = RW_QUAD * RW_HEAD_DIM
RW_SPLITS = (RW_WIDTH, 2 * RW_WIDTH, 3 * RW_WIDTH, 3 * RW_WIDTH + 2 * RW_DECAY_RANK,
             3 * RW_WIDTH + 2 * RW_DECAY_RANK + 2 * RW_A_RANK)
HALO = 8


def _split2(x):
    hi = x.astype(BF16)
    return hi, (x - hi.astype(F32)).astype(BF16)


def _dot3(x, w_hi, w_lo):
    hi, lo = _split2(x)
    return (jnp.dot(hi, w_hi, preferred_element_type=F32) + jnp.dot(lo, w_hi, preferred_element_type=F32)
            + jnp.dot(hi, w_lo, preferred_element_type=F32))


def _seg_sum(x, ones_bd):
    hi, lo = _split2(x)
    return jnp.dot(hi, ones_bd, preferred_element_type=F32) + jnp.dot(lo, ones_bd, preferred_element_type=F32)


def _rw_prep_kernel(p_ref, pp_ref, pn_ref, mu_ref, wup_hi, wup_lo, w0_ref, aup_hi, aup_lo, a0_ref,
                    gup_hi, gup_lo, kk_ref, ka_ref, ones_ref,
                    r_ref, v_ref, kkn_ref, g_ref, lw_ref, kd_ref, a_ref):
    j = pl.program_id(1)
    p = p_ref[0]
    tm = p.shape[0]
    prev_row = jnp.where(j > 0, pp_ref[0, HALO - 1:HALO, :], 0.0)
    next_row = jnp.where(j < pl.num_programs(1) - 1, pn_ref[0, 0:1, :], 0.0)
    row = lax.broadcasted_iota(jnp.int32, p.shape, 0)
    prev = jnp.where(row == 0, prev_row, pltpu.roll(p, 1, 0))
    nxt = jnp.where(row == tm - 1, next_row, pltpu.roll(p, tm - 1, 0))
    pm = p + mu_ref[...] * (0.5 * (prev + nxt) - p)
    s0, s1, s2, s3, s4 = RW_SPLITS
    r, k, v = pm[:, 0:s0], pm[:, s0:s1], pm[:, s1:s2]
    wd = jnp.tanh(pm[:, s2:s3])
    ad = pm[:, s3:s4]
    gd = pm[:, s4:]
    w_raw = w0_ref[...] + _dot3(wd, wup_hi[...], wup_lo[...])
    z = -w_raw
    softplus = jnp.maximum(z, 0.0) + jnp.log(1.0 + jnp.exp(-jnp.abs(z)))
    lw_ref[0] = -jnp.exp(-softplus - 0.5)
    a = jax.nn.sigmoid(a0_ref[...] + _dot3(ad, aup_hi[...], aup_lo[...]))
    a_ref[0] = a
    g_ref[0] = _dot3(jax.nn.sigmoid(gd), gup_hi[...], gup_lo[...])
    kk = k * kk_ref[...]
    kkn_ref[0] = kk * lax.rsqrt(_seg_sum(kk * kk, ones_ref[...]) + 1e-12)
    for d in range(2):
        sl = slice(d * RW_WIDTH, (d + 1) * RW_WIDTH)
        kd_ref[0, :, sl] = k * (1.0 + (a[:, sl] - 1.0) * ka_ref[...])
    r_ref[0] = r
    v_ref[0] = v


def rw_prep(p, lw, b, L):
    tm = _tile(L, 256)
    nh = L // HALO
    p3 = p.reshape(b, L, RW_IN)

    def full(a):
        return pl.BlockSpec(a.shape, lambda i, j: (0,) * a.ndim)

    weights = (lw["mu"], lw["wup_hi"], lw["wup_lo"], lw["w0"], lw["aup_hi"], lw["aup_lo"], lw["a0"],
               lw["gup_hi"], lw["gup_lo"], lw["k_k"], lw["k_a"], lw["ones_bd"])
    one = jax.ShapeDtypeStruct((b, L, RW_WIDTH), F32)
    two = jax.ShapeDtypeStruct((b, L, 2 * RW_WIDTH), F32)
    one_spec = pl.BlockSpec((1, tm, RW_WIDTH), lambda i, j: (i, j, 0))
    two_spec = pl.BlockSpec((1, tm, 2 * RW_WIDTH), lambda i, j: (i, j, 0))
    return pl.pallas_call(
        _rw_prep_kernel,
        grid=(b, L // tm),
        in_specs=[pl.BlockSpec((1, tm, RW_IN), lambda i, j: (i, j, 0)),
                  pl.BlockSpec((1, HALO, RW_IN), lambda i, j: (i, jnp.maximum(j * (tm // HALO) - 1, 0), 0)),
                  pl.BlockSpec((1, HALO, RW_IN), lambda i, j: (i, jnp.minimum((j + 1) * (tm // HALO), nh - 1), 0))]
        + [full(a) for a in weights],
        out_specs=[one_spec, one_spec, one_spec, one_spec, two_spec, two_spec, two_spec],
        out_shape=[one, one, one, one, two, two, two],
        compiler_params=_params("parallel", "parallel"),
        name="rw_prep",
    )(p3, p3, p3, *weights)


def _rw_masks(reverse):
    C, NQ = RW_CHUNK, RW_QUAD
    ti = lax.broadcasted_iota(jnp.int32, (C, C), 0)
    si = lax.broadcasted_iota(jnp.int32, (C, C), 1)
    tri = jnp.where((si >= ti) if reverse else (si <= ti), 1.0, 0.0).astype(BF16)
    wt = lax.broadcasted_iota(jnp.int32, (C, NQ * C), 0)
    ws = lax.broadcasted_iota(jnp.int32, (C, NQ * C), 1) & (C - 1)
    strict = (ws > wt) if reverse else (ws < wt)
    incl = (ws >= wt) if reverse else (ws <= wt)
    same = (ws // RW_SUB) == (wt // RW_SUB)
    eye = jnp.where(ws == wt, 1.0, 0.0)
    return tri, strict, incl, same, eye


def _rw_chunks(chains, bdm, bdm_b):
    C, NQ = RW_CHUNK, RW_QUAD
    n = len(chains)
    masks = {rev: _rw_masks(rev) for rev in sorted({c["reverse"] for c in chains})}
    tri = [masks[c["reverse"]][0] for c in chains]
    strict = [masks[c["reverse"]][1] for c in chains]
    incl = [masks[c["reverse"]][2] for c in chains]
    same = [masks[c["reverse"]][3] for c in chains]
    eye = [masks[c["reverse"]][4] for c in chains]
    ids = range(n)

    def dot(a, b):
        return jnp.dot(a, b, preferred_element_type=F32)

    def dot_nt(a, b):
        return lax.dot_general(a, b, (((1,), (1,)), ((), ())), preferred_element_type=F32)

    def bd(x):
        return jnp.concatenate([x] * NQ, axis=0) * bdm_b

    def mm(ms, xs):
        return [dot(m.astype(BF16), bd(x.astype(BF16))) for m, x in zip(ms, xs)]

    lw = [c["lw"] for c in chains]
    hi = [x.astype(BF16) for x in lw]
    r1 = [x - h.astype(F32) for x, h in zip(lw, hi)]
    mid = [x.astype(BF16) for x in r1]
    lo = [(x - m.astype(F32)).astype(BF16) for x, m in zip(r1, mid)]
    cum = [dot(tri[i], hi[i]) + dot(tri[i], mid[i]) + dot(tri[i], lo[i]) for i in ids]
    tot = [cum[i][0:1] if chains[i]["reverse"] else cum[i][C - 1:C] for i in ids]
    pinv = [jnp.exp(-x) for x in cum]
    pend = [jnp.exp(t - x) for t, x in zip(tot, cum)]
    kka = [c["kk"] * c["a"] for c in chains]
    kap = [(chains[i]["kk"] * jnp.exp(cum[i] - lw[i])).astype(BF16) for i in ids]
    bet = [(kka[i] * pinv[i]).astype(BF16) for i in ids]
    kt = [(chains[i]["kd"] * pinv[i]).astype(BF16) for i in ids]
    rt = [(chains[i]["r"] * jnp.exp(cum[i])).astype(BF16) for i in ids]
    v = [c["v"] for c in chains]

    x2 = [jnp.concatenate([kap[i], rt[i]], axis=0) for i in ids]
    y2 = [jnp.concatenate([bd(bet[i]), bd(kt[i])], axis=0) for i in ids]
    aw = [dot_nt(x2[i], y2[i]) for i in ids]
    zz = [dot_nt(x2[i], chains[i]["zt"].astype(BF16)) for i in ids]
    a_ab = [jnp.where(strict[i], aw[i][:C, :NQ * C], 0.0) for i in ids]
    a_ak = [jnp.where(strict[i], aw[i][:C, NQ * C:], 0.0) for i in ids]
    a_rb = [jnp.where(incl[i], aw[i][C:, :NQ * C], 0.0) for i in ids]
    a_rk = [jnp.where(incl[i], aw[i][C:, NQ * C:], 0.0) for i in ids]
    akv = mm(a_ak, v)
    rhs = [-(zz[i][:C] + akv[i]) for i in ids]
    d = [jnp.where(same[i], a_ab[i], 0.0) for i in ids]
    e = [a_ab[i] - d[i] for i in ids]
    d2 = mm(d, d)
    d4 = mm(d2, d2)
    d8 = mm(d4, d4)
    t = [eye[i] - d[i] for i in ids]
    t = [x + y for x, y in zip(t, mm(t, d2))]
    t = [x + y for x, y in zip(t, mm(t, d4))]
    t = [x + y for x, y in zip(t, mm(t, d8))]
    nn = mm(t, e)
    n2 = mm(nn, nn)
    u = mm(t, rhs)
    u = [x + y for x, y in zip(u, mm(n2, u))]
    u = [x - y for x, y in zip(u, mm(nn, u))]
    yv = mm(a_rk, v)
    yu = mm(a_rb, u)
    y = [zz[i][C:] + yv[i] + yu[i] for i in ids]
    lhs = [jnp.concatenate([v[i], u[i]], axis=0).astype(BF16) for i in ids]
    rhs2 = [jnp.concatenate([chains[i]["kd"] * pend[i], kka[i] * pend[i]], axis=0).astype(BF16) for i in ids]
    upd = [lax.dot_general(lhs[i], rhs2[i], (((0,), (0,)), ((), ())), preferred_element_type=F32) for i in ids]
    zt = [chains[i]["zt"] * jnp.exp(tot[i]) + upd[i] * bdm for i in ids]
    return list(zip(y, zt))


def _rw_scan_kernel(rf, vf, kkf, lwf, kdf, af, rb, vb, kkb, lwb, kdb, ab, bdm_ref, yf_ref, yb_ref, zt_ref):
    @pl.when(pl.program_id(1) == 0)
    def _():
        zt_ref[...] = jnp.zeros(zt_ref.shape, F32)

    bdm = bdm_ref[...]
    nq = RW_HEADS // RW_QUAD
    names = ("r", "lw", "kd", "v", "kk", "a")
    chains, outs = [], []
    for di, (reverse, refs, y_ref) in enumerate(((False, (rf, lwf, kdf, vf, kkf, af), yf_ref),
                                                 (True, (rb, lwb, kdb, vb, kkb, ab), yb_ref))):
        for q in range(nq):
            sl = slice(q * RW_QW, (q + 1) * RW_QW)
            chain = {name: ref[0, :, sl] for name, ref in zip(names, refs)}
            chain["zt"] = zt_ref[di * nq + q]
            chain["reverse"] = reverse
            chains.append(chain)
            outs.append((y_ref, sl, di * nq + q))
    for (y, zt), (y_ref, sl, idx) in zip(_rw_chunks(chains, bdm, bdm.astype(BF16)), outs):
        y_ref[0, :, sl] = y
        zt_ref[idx] = zt


def rw_scan(r, v, kk, lw, kd, a, bdm):
    b, L, _ = r.shape
    nc = L // RW_CHUNK
    blk = (1, RW_CHUNK, RW_WIDTH)
    f1 = pl.BlockSpec(blk, lambda bi, c: (bi, c, 0))
    b1 = pl.BlockSpec(blk, lambda bi, c: (bi, nc - 1 - c, 0))
    b2 = pl.BlockSpec(blk, lambda bi, c: (bi, nc - 1 - c, 1))
    out = jax.ShapeDtypeStruct((b, L, RW_WIDTH), F32)
    return pl.pallas_call(
        _rw_scan_kernel,
        grid=(b, nc),
        in_specs=[f1, f1, f1, f1, f1, f1, b1, b1, b1, b2, b2, b2,
                  pl.BlockSpec(bdm.shape, lambda bi, c: (0, 0))],
        out_specs=[f1, b1],
        out_shape=[out, out],
        scratch_shapes=[pltpu.VMEM((2 * RW_HEADS // RW_QUAD, RW_QW, RW_QW), F32)],
        compiler_params=_params("parallel", "arbitrary"),
        name="rw_scan",
    )(r, v, kk, lw, kd, a, r, v, kk, lw, kd, a, bdm)


def _rw_post_kernel(yf_ref, yb_ref, r_ref, v_ref, kd_ref, g_ref, lnw_ref, lnb_ref, rk_ref, ones_ref, o_ref):
    ones_bd = ones_ref[...]
    inv_n = 1.0 / RW_HEAD_DIM
    y = yf_ref[...] + yb_ref[...]
    yc = y - _seg_sum(y, ones_bd) * inv_n
    var = _seg_sum(yc * yc, ones_bd) * inv_n
    y = yc * lax.rsqrt(var + RW_LN_EPS) * lnw_ref[...] + lnb_ref[...]
    kd = kd_ref[:, 0:RW_WIDTH] + kd_ref[:, RW_WIDTH:]
    bonus = _seg_sum(r_ref[...] * kd * rk_ref[...], ones_bd)
    o_ref[...] = ((y + bonus * v_ref[...]) * g_ref[...]).astype(o_ref.dtype)


def rw_post(yf, yb, r, v, kd, g, lw):
    t = yf.shape[0]
    tm = _tile(t, 512)
    one = pl.BlockSpec((tm, RW_WIDTH), lambda i: (i, 0))
    two = pl.BlockSpec((tm, 2 * RW_WIDTH), lambda i: (i, 0))

    def full(a):
        return pl.BlockSpec(a.shape, lambda i: (0,) * a.ndim)

    weights = (lw["ln_w"], lw["ln_b"], lw["r_k"], lw["ones_bd"])
    return pl.pallas_call(
        _rw_post_kernel,
        grid=(t // tm,),
        in_specs=[one, one, one, one, two, one] + [full(a) for a in weights],
        out_specs=one,
        out_shape=jax.ShapeDtypeStruct((t, RW_WIDTH), BF16),
        compiler_params=_params("parallel"),
        name="rw_post",
    )(yf, yb, r, v, kd, g, *weights)


def _block_diag2(w):
    z = jnp.zeros_like(w[0])
    return jnp.concatenate([jnp.concatenate([w[0], z], axis=1), jnp.concatenate([z, w[1]], axis=1)], axis=0)


def _hi_lo(w):
    w = w.astype(F32)
    hi = w.astype(BF16)
    return hi, (w - hi.astype(F32)).astype(BF16)


def _rw_layer_weights(mu, w0, w_up, a0, a_up, g_up, k_k, k_a, r_k, ln_w, ln_b):
    wup_hi, wup_lo = _hi_lo(_block_diag2(w_up))
    aup_hi, aup_lo = _hi_lo(_block_diag2(a_up))
    gup_hi, gup_lo = _hi_lo(g_up)
    head = np.arange(RW_WIDTH) // RW_HEAD_DIM

    def row(t):
        return t.reshape(1, -1).astype(F32)

    return {
        "mu": row(mu), "wup_hi": wup_hi, "wup_lo": wup_lo, "w0": row(w0),
        "aup_hi": aup_hi, "aup_lo": aup_lo, "a0": row(a0), "gup_hi": gup_hi, "gup_lo": gup_lo,
        "k_k": row(k_k), "k_a": row(k_a), "r_k": row(r_k), "ln_w": row(ln_w), "ln_b": row(ln_b),
        "ones_bd": jnp.asarray(head[:, None] == head[None, :], BF16),
    }


def rwkv7_mix(p, lw, bdm, b, L):
    r, v, kk, g, lwd, kd, a = rw_prep(p, lw, b, L)
    yf, yb = rw_scan(r, v, kk, lwd, kd, a, bdm)
    t = b * L

    def flat(x):
        return x.reshape(t, x.shape[-1])

    return rw_post(flat(yf), flat(yb), flat(r), flat(v), flat(kd), flat(g), lw)


def _merge_kernel(x_ref, ya_ref, yb_ref, yc_ref, gin_ref, bg_ref, pa_ref, pb_ref, pc_ref, wo_ref, o_ref):
    mixed = None
    for i, (y_ref, p_ref) in enumerate(((ya_ref, pa_ref), (yb_ref, pb_ref), (yc_ref, pc_ref))):
        sl = slice(i * D_MODEL, (i + 1) * D_MODEL)
        gate = jax.nn.sigmoid(gin_ref[:, sl].astype(F32) + bg_ref[:, sl])
        term = gate * jnp.dot(y_ref[...], p_ref[...], preferred_element_type=F32)
        mixed = term if mixed is None else mixed + term
    o_ref[...] = x_ref[...] + jnp.dot(mixed.astype(BF16), wo_ref[...], preferred_element_type=F32)


def merge(x, ya, yb, yc, gate_in, b_gate, pa, pb, pc, wo):
    t = x.shape[0]
    tm = _tile(t, 512)

    def rows(w):
        return pl.BlockSpec((tm, w), lambda i: (i, 0))

    def full(a):
        return pl.BlockSpec(a.shape, lambda i: (0,) * a.ndim)

    return pl.pallas_call(
        _merge_kernel,
        grid=(t // tm,),
        in_specs=[rows(D_MODEL), rows(NA_WIDTH), rows(MLA_WIDTH), rows(RW_WIDTH), rows(N_BRANCH * D_MODEL),
                  full(b_gate), full(pa), full(pb), full(pc), full(wo)],
        out_specs=rows(D_MODEL),
        out_shape=jax.ShapeDtypeStruct((t, D_MODEL), F32),
        compiler_params=_params("parallel"),
        name="merge",
    )(x, ya, yb, yc, gate_in, b_gate, pa, pb, pc, wo)


def _ffn_kernel(x_ref, g_ref, wg_ref, wu_ref, wd_ref, o_ref, h_ref, acc_ref):
    j = pl.program_id(1)

    @pl.when(j == 0)
    def _():
        x = x_ref[...]
        ms = jnp.mean(x * x, axis=-1, keepdims=True)
        h_ref[...] = (x * lax.rsqrt(ms + NORM_EPS) * g_ref[...]).astype(BF16)
        acc_ref[...] = x

    h = h_ref[...]
    gate = jnp.dot(h, wg_ref[...], preferred_element_type=F32)
    up = jnp.dot(h, wu_ref[...], preferred_element_type=F32)
    act = (gate * jax.nn.sigmoid(gate) * up).astype(BF16)
    acc_ref[...] += jnp.dot(act, wd_ref[...], preferred_element_type=F32)

    @pl.when(j == pl.num_programs(1) - 1)
    def _():
        o_ref[...] = acc_ref[...]


def ffn(x, g, wg, wu, wd, tm_pref=1024, tf_pref=256):
    t, d = x.shape
    f = wg.shape[1]
    tm = _tile(t, tm_pref)
    tf = _tile(f, tf_pref, LANES)
    return pl.pallas_call(
        _ffn_kernel,
        grid=(t // tm, f // tf),
        in_specs=[pl.BlockSpec((tm, d), lambda i, j: (i, 0)),
                  pl.BlockSpec((1, d), lambda i, j: (0, 0)),
                  pl.BlockSpec((d, tf), lambda i, j: (0, j)),
                  pl.BlockSpec((d, tf), lambda i, j: (0, j)),
                  pl.BlockSpec((tf, d), lambda i, j: (j, 0))],
        out_specs=pl.BlockSpec((tm, d), lambda i, j: (i, 0)),
        out_shape=jax.ShapeDtypeStruct((t, d), F32),
        scratch_shapes=[pltpu.VMEM((tm, d), BF16), pltpu.VMEM((tm, d), F32)],
        compiler_params=_params("parallel", "arbitrary"),
        name="ffn",
    )(x, g.reshape(1, d), wg, wu, wd)


IN_SIZES = (NA_WIDTH, NA_WIDTH, NA_WIDTH, MLA_Q_RANK, MLA_KV_RANK, MLA_ROPE, RW_IN, N_BRANCH * D_MODEL)
IN_SPLITS = tuple(int(s) for s in np.cumsum(IN_SIZES)[:-1])


def _layer_weights(l, norm1_g, w_in, b_gate, na_q_norm, na_k_norm, na_rpb, na_proj,
                   mla_cq_norm, mla_ckv_norm, mla_w_uq, mla_w_ukv, mla_q_norm, mla_k_norm, mla_proj,
                   rw_mu, rw_w0, rw_w_up, rw_a0, rw_a_up, rw_g_up, rw_k_k, rw_k_a, rw_r_k, rw_ln_w, rw_ln_b, rw_proj,
                   w_out, norm2_g, ffn_w_gate, ffn_w_up, ffn_w_down):
    w = w_in[l]
    kr_lo = MLA_NOPE
    w_kr = jnp.pad(w[:, IN_SPLITS[4]:IN_SPLITS[5]], ((0, 0), (kr_lo, MLA_HEAD_PAD - kr_lo - MLA_ROPE)))
    w_attn = jnp.concatenate([w[:, :IN_SPLITS[4]], w_kr], axis=1).astype(BF16)
    return {
        "norm1_g": norm1_g[l], "w_attn": w_attn,
        "w_rw": w[:, IN_SPLITS[5]:IN_SPLITS[6]].astype(BF16),
        "w_gate": w[:, IN_SPLITS[6]:].astype(BF16),
        "b_gate": b_gate[l].reshape(1, -1).astype(F32),
        "na_gq": na_q_norm[l].astype(F32) * (NA_HEAD_DIM ** -0.5), "na_gk": na_k_norm[l].astype(F32),
        "na_bias": _na_bias_table(na_rpb[l]),
        "na_proj": na_proj[l].astype(BF16), "mla_proj": mla_proj[l].astype(BF16), "rw_proj": rw_proj[l].astype(BF16),
        "mla": _mla_layer_weights(mla_cq_norm[l], mla_ckv_norm[l], mla_w_uq[l], mla_w_ukv[l], mla_q_norm[l], mla_k_norm[l]),
        "rw": _rw_layer_weights(rw_mu[l], rw_w0[l], rw_w_up[l], rw_a0[l], rw_a_up[l], rw_g_up[l],
                                rw_k_k[l], rw_k_a[l], rw_r_k[l], rw_ln_w[l], rw_ln_b[l]),
        "w_out": w_out[l].astype(BF16), "norm2_g": norm2_g[l],
        "ffn_w_gate": ffn_w_gate[l].astype(BF16), "ffn_w_up": ffn_w_up[l].astype(BF16),
        "ffn_w_down": ffn_w_down[l].astype(BF16),
    }


def _trunk(x, layers):
    b, L, d = x.shape
    t = b * L
    x = x.reshape(t, d)
    tabs = _rope_tables(L)
    head = np.arange(RW_QW) // RW_HEAD_DIM
    bdm = jnp.asarray(head[:, None] == head[None, :], F32)
    for lw in layers:
        attn_in = norm_matmul(x, lw["norm1_g"], lw["w_attn"], BF16)
        rw_in = norm_matmul(x, lw["norm1_g"], lw["w_rw"], F32)
        gate_in = norm_matmul(x, lw["norm1_g"], lw["w_gate"], BF16)
        qa, ka, va = na_prep(attn_in, lw["na_gq"], lw["na_gk"], b, L)
        y_a = na_attention(qa, ka, va, lw["na_bias"]).reshape(t, NA_WIDTH)
        qm, km, vm = mla_prep(attn_in, lw["mla"], tabs, b, L)
        y_b = mla_flash(qm, km, vm).reshape(t, MLA_WIDTH)
        y_c = rwkv7_mix(rw_in, lw["rw"], bdm, b, L)
        x = merge(x, y_a, y_b, y_c, gate_in, lw["b_gate"], lw["na_proj"], lw["mla_proj"], lw["rw_proj"], lw["w_out"])
        x = ffn(x, lw["norm2_g"], lw["ffn_w_gate"], lw["ffn_w_up"], lw["ffn_w_down"])
    return x.reshape(b, L, d)


def kernel(x_prompt, x_sample, norm1_g, w_in, b_gate, na_q_norm, na_k_norm, na_rpb, na_proj, mla_cq_norm, mla_ckv_norm, mla_w_uq, mla_w_ukv, mla_q_norm, mla_k_norm, mla_proj, rw_mu, rw_w0, rw_w_up, rw_a0, rw_a_up, rw_g_up, rw_k_k, rw_k_a, rw_r_k, rw_ln_w, rw_ln_b, rw_proj, w_out, norm2_g, ffn_w_gate, ffn_w_up, ffn_w_down):
    weights = (norm1_g, w_in, b_gate, na_q_norm, na_k_norm, na_rpb, na_proj,
               mla_cq_norm, mla_ckv_norm, mla_w_uq, mla_w_ukv, mla_q_norm, mla_k_norm, mla_proj,
               rw_mu, rw_w0, rw_w_up, rw_a0, rw_a_up, rw_g_up, rw_k_k, rw_k_a, rw_r_k, rw_ln_w, rw_ln_b, rw_proj,
               w_out, norm2_g, ffn_w_gate, ffn_w_up, ffn_w_down)
    layers = [_layer_weights(l, *weights) for l in range(norm1_g.shape[0])]
    return (_trunk(x_prompt, layers), _trunk(x_sample, layers))
```

```python
import functools

import jax
import jax.numpy as jnp
import numpy as np
from jax import lax
from jax.experimental import pallas as pl
from jax.experimental.pallas import tpu as pltpu

F32 = jnp.float32
BF16 = jnp.bfloat16

D_MODEL = 1024
GRID_W = 64
N_BRANCH = 3
NORM_EPS = 1e-6

NA_HEADS = 8
NA_HEAD_DIM = 64
NA_WIDTH = NA_HEADS * NA_HEAD_DIM
NA_WIN_ROWS = 8
NA_WIN_COLS = 16
NA_RPB_ROWS = 2 * NA_WIN_ROWS - 1
NA_RPB_COLS = 2 * NA_WIN_COLS - 1
NA_MASK = -1e30

MLA_HEADS = 8
MLA_NOPE = 64
MLA_ROPE = 32
MLA_QK = MLA_NOPE + MLA_ROPE
MLA_V = 64
MLA_WIDTH = MLA_HEADS * MLA_V
MLA_Q_RANK = 256
MLA_KV_RANK = 128
MLA_HEAD_PAD = 128
ROPE_THETA = 10000.0

RW_HEADS = 8
RW_HEAD_DIM = 64
RW_WIDTH = RW_HEADS * RW_HEAD_DIM
RW_DECAY_RANK = 64
RW_A_RANK = 64
RW_G_RANK = 128
RW_LN_EPS = 64e-5
RW_IN = 3 * RW_WIDTH + 2 * RW_DECAY_RANK + 2 * RW_A_RANK + RW_G_RANK

D_FF = 2816

VMEM_LIMIT_BYTES = 56 * 1024 * 1024
LANES = 128


def _params(*sem):
    return pltpu.CompilerParams(dimension_semantics=sem, vmem_limit_bytes=VMEM_LIMIT_BYTES)


def _tile(n, pref, mult=8):
    if n <= pref:
        return n
    t = (pref // mult) * mult
    while t >= mult:
        if n % t == 0:
            return t
        t -= mult
    return n


def _resident(a):
    return pl.BlockSpec(a.shape, lambda *_: (0,) * a.ndim, pipeline_mode=pl.Buffered(1))


def _in_proj_kernel(x_ref, g_ref, *refs):
    n = len(refs) // 2
    x = x_ref[...]
    ms = jnp.mean(x * x, axis=-1, keepdims=True)
    h = (x * lax.rsqrt(ms + NORM_EPS) * g_ref[...]).astype(BF16)
    for w_ref, o_ref in zip(refs[:n], refs[n:]):
        o_ref[...] = jnp.dot(h, w_ref[...], preferred_element_type=F32).astype(o_ref.dtype)


def in_proj(x, g, weights, out_dtypes, tm_pref=512):
    t, d = x.shape
    tm = _tile(t, tm_pref)
    g = g.reshape(1, d)
    return pl.pallas_call(
        _in_proj_kernel,
        grid=(t // tm,),
        in_specs=[pl.BlockSpec((tm, d), lambda i: (i, 0)), _resident(g)] + [_resident(w) for w in weights],
        out_specs=[pl.BlockSpec((tm, w.shape[1]), lambda i: (i, 0)) for w in weights],
        out_shape=[jax.ShapeDtypeStruct((t, w.shape[1]), dt) for w, dt in zip(weights, out_dtypes)],
        compiler_params=_params("parallel"),
        name="in_proj",
    )(x, g, *weights)


def _split2(x):
    hi = x.astype(BF16)
    return hi, (x - hi.astype(F32)).astype(BF16)


def _seg_sum(x, ones_bd):
    hi, lo = _split2(x)
    return jnp.dot(hi, ones_bd, preferred_element_type=F32) + jnp.dot(lo, ones_bd, preferred_element_type=F32)


def _head_ones(width, head_dim):
    head = np.arange(width) // head_dim
    return jnp.asarray(head[:, None] == head[None, :], BF16)


def _na_prep_kernel(x_ref, gq_ref, gk_ref, ones_ref, q_ref, k_ref, v_ref):
    inv_n = 1.0 / NA_HEAD_DIM

    def normed(t, g):
        return (t * lax.rsqrt(_seg_sum(t * t, ones_ref[...]) * inv_n + NORM_EPS) * g).astype(BF16)

    q = normed(x_ref[0, :, 0:NA_WIDTH].astype(F32), gq_ref[...])
    k = normed(x_ref[0, :, NA_WIDTH:2 * NA_WIDTH].astype(F32), gk_ref[...])
    for h in range(NA_HEADS):
        sl = slice(h * NA_HEAD_DIM, (h + 1) * NA_HEAD_DIM)
        q_ref[0, h] = q[:, sl]
        k_ref[0, h] = k[:, sl]
        v_ref[0, h] = x_ref[0, :, 2 * NA_WIDTH + sl.start:2 * NA_WIDTH + sl.stop]


def na_prep(na_in, gq, gk, b, L):
    tm = _tile(L, 512)
    x3 = na_in.reshape(b, L, 3 * NA_WIDTH)
    hm = jax.ShapeDtypeStruct((b, NA_HEADS, L, NA_HEAD_DIM), BF16)
    hm_spec = pl.BlockSpec((1, NA_HEADS, tm, NA_HEAD_DIM), lambda i, j: (i, 0, j, 0))
    consts = (jnp.tile(gq, NA_HEADS).reshape(1, -1), jnp.tile(gk, NA_HEADS).reshape(1, -1),
              _head_ones(NA_WIDTH, NA_HEAD_DIM))
    return pl.pallas_call(
        _na_prep_kernel,
        grid=(b, L // tm),
        in_specs=[pl.BlockSpec((1, tm, 3 * NA_WIDTH), lambda i, j: (i, j, 0))] + [_resident(a) for a in consts],
        out_specs=[hm_spec, hm_spec, hm_spec],
        out_shape=[hm, hm, hm],
        compiler_params=_params("parallel", "parallel"),
        name="na_prep",
    )(x3, *consts)


NA_ROWS_PER_STEP = 8
NA_ROWS_PER_ITER = 4
NA_BLOCK = NA_ROWS_PER_STEP * GRID_W
NA_KEYS = NA_WIN_ROWS * GRID_W


def _na_attn_kernel(q_ref, kp_ref, kc_ref, kn_ref, vp_ref, vc_ref, vn_ref, bias_ref, o_ref, kbuf, vbuf, *, rows):
    i = pl.program_id(1)
    kbuf[:, 0:NA_BLOCK] = kp_ref[0]
    kbuf[:, NA_BLOCK:2 * NA_BLOCK] = kc_ref[0]
    kbuf[:, 2 * NA_BLOCK:3 * NA_BLOCK] = kn_ref[0]
    vbuf[:, 0:NA_BLOCK] = vp_ref[0]
    vbuf[:, NA_BLOCK:2 * NA_BLOCK] = vc_ref[0]
    vbuf[:, 2 * NA_BLOCK:3 * NA_BLOCK] = vn_ref[0]

    def rows_body(j, carry):
        work = []
        for jj in range(NA_ROWS_PER_ITER):
            jr = j * NA_ROWS_PER_ITER + jj
            r = i * NA_ROWS_PER_STEP + jr
            rs = jnp.clip(r - NA_WIN_ROWS // 2, 0, rows - NA_WIN_ROWS)
            off = pl.multiple_of((rs - (i - 1) * NA_ROWS_PER_STEP) * GRID_W, GRID_W)
            qoff = pl.multiple_of(jr * GRID_W, GRID_W)
            work += [(h, off, r - rs, qoff) for h in range(NA_HEADS)]
        s = [lax.dot_general(q_ref[0, h, pl.ds(qoff, GRID_W), :], kbuf[h, pl.ds(off, NA_KEYS), :],
                             (((1,), (1,)), ((), ())), preferred_element_type=F32) + bias_ref[pat, h]
             for h, off, pat, qoff in work]
        p = [jnp.exp(x - jnp.max(x, axis=-1, keepdims=True)) for x in s]
        l = [jnp.sum(x, axis=-1, keepdims=True) for x in p]
        o = [jnp.dot(x.astype(BF16), vbuf[h, pl.ds(off, NA_KEYS), :], preferred_element_type=F32) / y
             for x, y, (h, off, _, _) in zip(p, l, work)]
        for x, (h, _, _, qoff) in zip(o, work):
            o_ref[0, pl.ds(qoff, GRID_W), h * NA_HEAD_DIM:(h + 1) * NA_HEAD_DIM] = x.astype(o_ref.dtype)
        return carry

    lax.fori_loop(0, NA_ROWS_PER_STEP // NA_ROWS_PER_ITER, rows_body, 0)


def _na_bias_table(rpb):
    pat = np.arange(NA_WIN_ROWS)[:, None]
    w = np.arange(NA_WIN_ROWS)[None, :]
    dr_idx = w - pat + NA_WIN_ROWS - 1
    qc = np.arange(GRID_W)[:, None]
    kc = np.arange(GRID_W)[None, :]
    win_start = np.clip(qc - NA_WIN_COLS // 2, 0, GRID_W - NA_WIN_COLS)
    ok = (kc >= win_start) & (kc < win_start + NA_WIN_COLS)
    dc_idx = np.clip(kc - qc + NA_WIN_COLS - 1, 0, NA_RPB_COLS - 1)
    t = rpb.astype(F32)[:, dr_idx][:, :, :, dc_idx]
    t = jnp.where(jnp.asarray(ok)[None, None, None], t, NA_MASK)
    t = jnp.transpose(t, (1, 0, 3, 2, 4))
    return t.reshape(NA_WIN_ROWS, NA_HEADS, GRID_W, NA_KEYS)


def na_attention(q, k, v, bias):
    b, _, L, _ = q.shape
    rows = L // GRID_W
    assert rows % NA_ROWS_PER_STEP == 0 and rows >= NA_WIN_ROWS
    nblk = rows // NA_ROWS_PER_STEP
    blk = (1, NA_HEADS, NA_BLOCK, NA_HEAD_DIM)
    cur = pl.BlockSpec(blk, lambda bi, i: (bi, 0, i, 0))
    prev = pl.BlockSpec(blk, lambda bi, i: (bi, 0, jnp.maximum(i - 1, 0), 0))
    nxt = pl.BlockSpec(blk, lambda bi, i: (bi, 0, jnp.minimum(i + 1, nblk - 1), 0))
    buf = pltpu.VMEM((NA_HEADS, 3 * NA_BLOCK, NA_HEAD_DIM), BF16)
    return pl.pallas_call(
        functools.partial(_na_attn_kernel, rows=rows),
        grid=(b, nblk),
        in_specs=[cur, prev, cur, nxt, prev, cur, nxt,
                  pl.BlockSpec(bias.shape, lambda bi, i: (0, 0, 0, 0))],
        out_specs=pl.BlockSpec((1, NA_BLOCK, NA_WIDTH), lambda bi, i: (bi, i, 0)),
        out_shape=jax.ShapeDtypeStruct((b, L, NA_WIDTH), BF16),
        scratch_shapes=[buf, buf],
        compiler_params=_params("parallel", "parallel"),
        name="na_attn",
    )(q, k, k, k, v, v, v, bias)


def _rope_tables(L):
    t = np.arange(L)
    row = (t // GRID_W).astype(np.float32)
    col = (t % GRID_W).astype(np.float32)
    n_freq = MLA_ROPE // 4
    inv_freq = jnp.asarray(ROPE_THETA, F32) ** (-jnp.arange(n_freq, dtype=F32) / n_freq)
    ang = jnp.concatenate([jnp.asarray(row)[:, None] * inv_freq, jnp.asarray(col)[:, None] * inv_freq], axis=-1)
    cos, sin = jnp.cos(ang), jnp.sin(ang)
    ones = jnp.ones((L, MLA_NOPE), F32)
    zpad = jnp.zeros((L, MLA_HEAD_PAD - MLA_QK), F32)
    znope = jnp.zeros((L, MLA_NOPE), F32)
    c_tab = jnp.concatenate([ones, cos, cos, zpad], axis=-1)
    s_tab = jnp.concatenate([znope, sin, sin, zpad], axis=-1)
    return c_tab, s_tab


def _mla_prep_kernel(x_ref, gcq_ref, gckv_ref, wuq_ref, wuqr_ref, wuk_ref, wuv_ref, vones_ref, gq_ref, gk_ref,
                     c_ref, s_ref, q_ref, k_ref, v_ref):
    lo = MLA_Q_RANK + MLA_KV_RANK
    cq = x_ref[0, :, 0:MLA_Q_RANK].astype(F32)
    ckv = x_ref[0, :, MLA_Q_RANK:lo].astype(F32)
    kr = x_ref[0, :, lo:lo + MLA_HEAD_PAD].astype(F32)
    kr_rot = x_ref[0, :, lo + MLA_HEAD_PAD:].astype(F32)
    cqn = (cq * lax.rsqrt(jnp.mean(cq * cq, axis=-1, keepdims=True) + NORM_EPS) * gcq_ref[...]).astype(BF16)
    ckvn = (ckv * lax.rsqrt(jnp.mean(ckv * ckv, axis=-1, keepdims=True) + NORM_EPS) * gckv_ref[...]).astype(BF16)
    q_raw = jnp.dot(cqn, wuq_ref[...], preferred_element_type=F32)
    q_rot = jnp.dot(cqn, wuqr_ref[...], preferred_element_type=F32)
    k_raw = jnp.dot(ckvn, wuk_ref[...], preferred_element_type=F32)
    v_ref[0] = (jnp.dot(ckvn, wuv_ref[...], preferred_element_type=F32) + vones_ref[...]).astype(BF16)
    s = s_ref[...]
    gqc = gq_ref[...] * c_ref[...]
    gkc = gk_ref[...] * c_ref[...]
    kr_rot_s = kr_rot * s

    def norm_rope(t, gc, rot_s):
        inv = lax.rsqrt(jnp.sum(t * t, axis=-1, keepdims=True) * (1.0 / MLA_QK) + NORM_EPS)
        return (inv * (t * gc + rot_s)).astype(BF16)

    for h in range(MLA_HEADS):
        sl = slice(h * MLA_HEAD_PAD, (h + 1) * MLA_HEAD_PAD)
        q_ref[0, :, sl] = norm_rope(q_raw[:, sl], gqc, q_rot[:, sl] * s)
        k_ref[0, :, sl] = norm_rope(k_raw[:, sl] + kr, gkc, kr_rot_s)


def mla_prep(mla_in, lw, tabs, b, L):
    tm = _tile(L, 512)
    width = MLA_HEADS * MLA_HEAD_PAD
    in_w = mla_in.shape[-1]
    x3 = mla_in.reshape(b, L, in_w)
    tab_spec = pl.BlockSpec((tm, MLA_HEAD_PAD), lambda i, j: (j, 0))
    weights = (lw["gcq"], lw["gckv"], lw["wuq"], lw["wuq_rot"], lw["wuk"], lw["wuv"], lw["v_ones"], lw["gq"], lw["gk"])
    slab_spec = pl.BlockSpec((1, tm, width), lambda i, j: (i, j, 0))
    slab = jax.ShapeDtypeStruct((b, L, width), BF16)
    return pl.pallas_call(
        _mla_prep_kernel,
        grid=(b, L // tm),
        in_specs=[pl.BlockSpec((1, tm, in_w), lambda i, j: (i, j, 0))]
        + [_resident(a) for a in weights] + [tab_spec] * 2,
        out_specs=[slab_spec, slab_spec, slab_spec],
        out_shape=[slab, slab, slab],
        compiler_params=_params("parallel", "parallel"),
        name="mla_prep",
    )(x3, *weights, *tabs)


def _mla_flash_kernel(q_ref, k_ref, v_ref, o_ref, m_ref, acc_ref):
    ki = pl.program_id(2)
    tk = k_ref.shape[1]

    @pl.when(ki == 0)
    def _():
        m_ref[...] = jnp.full(m_ref.shape, -jnp.inf, F32)
        acc_ref[...] = jnp.zeros(acc_ref.shape, F32)

    for h in range(MLA_HEADS):
        sl = slice(h * MLA_HEAD_PAD, (h + 1) * MLA_HEAD_PAD)
        s = lax.dot_general(q_ref[0, :, sl], k_ref[0, :, sl], (((1,), (1,)), ((), ())), preferred_element_type=F32)
        m_prev = m_ref[h]
        m_next = jnp.maximum(m_prev, jnp.max(s, axis=-1, keepdims=True))
        alpha = jnp.exp2(m_prev - m_next)
        p = jnp.exp2(s - jnp.concatenate([m_next] * (tk // LANES), axis=1))
        acc_ref[h] = alpha * acc_ref[h] + jnp.dot(p.astype(BF16), v_ref[0, :, sl], preferred_element_type=F32)
        m_ref[h] = m_next

    @pl.when(ki == pl.num_programs(2) - 1)
    def _():
        lane = lax.broadcasted_iota(jnp.int32, (acc_ref.shape[1], LANES), 1)

        def normalised(h):
            acc = acc_ref[h]
            return acc / pltpu.roll(acc, MLA_V, 1)

        for hp in range(MLA_HEADS // 2):
            pair = jnp.where(lane < MLA_V, normalised(2 * hp), pltpu.roll(normalised(2 * hp + 1), MLA_V, 1))
            o_ref[0, :, hp * LANES:(hp + 1) * LANES] = pair.astype(o_ref.dtype)


def mla_flash(q, k, v, tq_pref=512, tk_pref=2048):
    b, L, width = q.shape
    tq = _tile(L, tq_pref)
    tk = _tile(L, tk_pref, LANES)
    return pl.pallas_call(
        _mla_flash_kernel,
        grid=(b, L // tq, L // tk),
        in_specs=[pl.BlockSpec((1, tq, width), lambda bi, qi, ki: (bi, qi, 0)),
                  pl.BlockSpec((1, tk, width), lambda bi, qi, ki: (bi, ki, 0)),
                  pl.BlockSpec((1, tk, width), lambda bi, qi, ki: (bi, ki, 0))],
        out_specs=pl.BlockSpec((1, tq, MLA_WIDTH), lambda bi, qi, ki: (bi, qi, 0)),
        out_shape=jax.ShapeDtypeStruct((b, L, MLA_WIDTH), BF16),
        scratch_shapes=[pltpu.VMEM((MLA_HEADS, tq, LANES), F32),
                        pltpu.VMEM((MLA_HEADS, tq, LANES), F32)],
        compiler_params=_params("parallel", "parallel", "arbitrary"),
        name="mla_flash",
    )(q, k, v)


def _rope_partner(w, g):
    half = MLA_ROPE // 2
    wg = w.astype(F32) * g
    return jnp.concatenate([-wg[..., half:], wg[..., :half]], axis=-1)


def _mla_layer_weights(cq_norm, ckv_norm, w_uq, w_ukv, q_norm, k_norm):
    pad = MLA_HEAD_PAD - MLA_QK
    gq = q_norm.astype(F32) * (MLA_QK ** -0.5 * np.log2(np.e))
    wuq3 = w_uq.reshape(MLA_Q_RANK, MLA_HEADS, MLA_QK)
    wuq = jnp.pad(wuq3, ((0, 0), (0, 0), (0, pad)))
    wuq_rot = jnp.pad(_rope_partner(wuq3[:, :, MLA_NOPE:], gq[MLA_NOPE:]), ((0, 0), (0, 0), (MLA_NOPE, pad)))
    wukv = w_ukv.reshape(MLA_KV_RANK, MLA_HEADS, MLA_NOPE + MLA_V)
    wuk = jnp.pad(wukv[:, :, :MLA_NOPE], ((0, 0), (0, 0), (0, MLA_HEAD_PAD - MLA_NOPE)))
    wuv = jnp.pad(wukv[:, :, MLA_NOPE:], ((0, 0), (0, 0), (0, MLA_HEAD_PAD - MLA_V)))
    v_ones = np.tile(np.arange(MLA_HEAD_PAD) >= MLA_V, MLA_HEADS).astype(np.float32)
    return {
        "v_ones": jnp.asarray(v_ones).reshape(1, -1),
        "gcq": cq_norm.reshape(1, -1).astype(F32),
        "gckv": ckv_norm.reshape(1, -1).astype(F32),
        "wuq": wuq.reshape(MLA_Q_RANK, -1).astype(BF16),
        "wuq_rot": wuq_rot.reshape(MLA_Q_RANK, -1).astype(BF16),
        "wuk": wuk.reshape(MLA_KV_RANK, -1).astype(BF16),
        "wuv": wuv.reshape(MLA_KV_RANK, -1).astype(BF16),
        "gq": jnp.pad(gq, (0, pad)).reshape(1, -1),
        "gk": jnp.pad(k_norm.astype(F32), (0, pad)).reshape(1, -1),
    }


RW_CHUNK = 64
RW_SUB = 16
RW_QUAD = 4
RW_QW = RW_QUAD * RW_HEAD_DIM
RW_SPLITS = (RW_WIDTH, 2 * RW_WIDTH, 3 * RW_WIDTH, 3 * RW_WIDTH + 2 * RW_DECAY_RANK,
             3 * RW_WIDTH + 2 * RW_DECAY_RANK + 2 * RW_A_RANK)
HALO = 8


def _dot3(x, w_hi, w_lo):
    hi, lo = _split2(x)
    return (jnp.dot(hi, w_hi, preferred_element_type=F32) + jnp.dot(lo, w_hi, preferred_element_type=F32)
            + jnp.dot(hi, w_lo, preferred_element_type=F32))


def _chunk_cumsum(x, tri):
    hi = x.astype(BF16)
    r1 = x - hi.astype(F32)
    mid = r1.astype(BF16)
    lo = (r1 - mid.astype(F32)).astype(BF16)
    return (jnp.dot(tri, hi, preferred_element_type=F32) + jnp.dot(tri, mid, preferred_element_type=F32)
            + jnp.dot(tri, lo, preferred_element_type=F32))


def _rw_prep_kernel(p_ref, pp_ref, pn_ref, mu_ref, wup_hi, wup_lo, w0_ref, aup_hi, aup_lo, a0_ref,
                    gup_hi, gup_lo, kk_ref, ka_ref, ones_ref, trif_ref, trib_ref,
                    r_ref, v_ref, kkn_ref, g_ref, lw_ref, kd_ref, a_ref, cum_ref):
    j = pl.program_id(1)
    p = p_ref[0]
    tm = p.shape[0]
    prev_row = jnp.where(j > 0, pp_ref[0, HALO - 1:HALO, :], 0.0)
    next_row = jnp.where(j < pl.num_programs(1) - 1, pn_ref[0, 0:1, :], 0.0)
    row = lax.broadcasted_iota(jnp.int32, p.shape, 0)
    prev = jnp.where(row == 0, prev_row, pltpu.roll(p, 1, 0))
    nxt = jnp.where(row == tm - 1, next_row, pltpu.roll(p, tm - 1, 0))
    pm = p + mu_ref[...] * (0.5 * (prev + nxt) - p)
    s0, s1, s2, s3, s4 = RW_SPLITS
    r, k, v = pm[:, 0:s0], pm[:, s0:s1], pm[:, s1:s2]
    wd = jnp.tanh(pm[:, s2:s3])
    ad = pm[:, s3:s4]
    gd = pm[:, s4:]
    w_raw = w0_ref[...] + _dot3(wd, wup_hi[...], wup_lo[...])
    z = -w_raw
    softplus = jnp.maximum(z, 0.0) + jnp.log(1.0 + jnp.exp(-jnp.abs(z)))
    lw = -jnp.exp(-softplus - 0.5)
    lw_ref[0] = lw
    cum_ref[0, :, 0:RW_WIDTH] = _chunk_cumsum(lw[:, 0:RW_WIDTH], trif_ref[...])
    cum_ref[0, :, RW_WIDTH:] = _chunk_cumsum(lw[:, RW_WIDTH:], trib_ref[...])
    a =jax.nn.sigmoid(a0_ref[...] + _dot3(ad, aup_hi[...], aup_lo[...]))
    a_ref[0] = a
    g_ref[0] = _dot3(jax.nn.sigmoid(gd), gup_hi[...], gup_lo[...])
    kk = k * kk_ref[...]
    kkn_ref[0] = kk * lax.rsqrt(_seg_sum(kk * kk, ones_ref[...]) + 1e-12)
    for d in range(2):
        sl = slice(d * RW_WIDTH, (d + 1) * RW_WIDTH)
        kd_ref[0, :, sl] = k * (1.0 + (a[:, sl] - 1.0) * ka_ref[...])
    r_ref[0] = r
    v_ref[0] = v


def rw_prep(p, lw, b, L):
    tm = _tile(L, 256, RW_CHUNK)
    nh = L // HALO
    p3 = p.reshape(b, L, RW_IN)
    t_i = np.arange(tm)[:, None]
    s_i = np.arange(tm)[None, :]
    same_chunk = (t_i // RW_CHUNK) == (s_i // RW_CHUNK)
    tri_f = jnp.asarray(same_chunk & (s_i <= t_i), BF16)
    tri_b = jnp.asarray(same_chunk & (s_i >= t_i), BF16)
    weights = (lw["mu"], lw["wup_hi"], lw["wup_lo"], lw["w0"], lw["aup_hi"], lw["aup_lo"], lw["a0"],
               lw["gup_hi"], lw["gup_lo"], lw["k_k"], lw["k_a"], lw["ones_bd"], tri_f, tri_b)
    one = jax.ShapeDtypeStruct((b, L, RW_WIDTH), F32)
    two = jax.ShapeDtypeStruct((b, L, 2 * RW_WIDTH), F32)
    one_spec = pl.BlockSpec((1, tm, RW_WIDTH), lambda i, j: (i, j, 0))
    two_spec = pl.BlockSpec((1, tm, 2 * RW_WIDTH), lambda i, j: (i, j, 0))
    return pl.pallas_call(
        _rw_prep_kernel,
        grid=(b, L // tm),
        in_specs=[pl.BlockSpec((1, tm, RW_IN), lambda i, j: (i, j, 0)),
                  pl.BlockSpec((1, HALO, RW_IN), lambda i, j: (i, jnp.maximum(j * (tm // HALO) - 1, 0), 0)),
                  pl.BlockSpec((1, HALO, RW_IN), lambda i, j: (i, jnp.minimum((j + 1) * (tm // HALO), nh - 1), 0))]
        + [_resident(a) for a in weights],
        out_specs=[one_spec, one_spec, one_spec, one_spec, two_spec, two_spec, two_spec, two_spec],
        out_shape=[one, one, one, one, two, two, two, two],
        compiler_params=_params("parallel", "parallel"),
        name="rw_prep",
    )(p3, p3, p3, *weights)


def _rw_masks(reverse):
    C, NQ = RW_CHUNK, RW_QUAD
    wt = lax.broadcasted_iota(jnp.int32, (C, NQ * C), 0)
    ws = lax.broadcasted_iota(jnp.int32, (C, NQ * C), 1) & (C - 1)
    strict = (ws > wt) if reverse else (ws < wt)
    incl = (ws >= wt) if reverse else (ws <= wt)
    same = (ws // RW_SUB) == (wt // RW_SUB)
    eye = jnp.where(ws == wt, 1.0, 0.0)
    return strict, incl, same, eye


def _rw_chunks(chains, bdm, bdm_b):
    C, NQ = RW_CHUNK, RW_QUAD
    n = len(chains)
    masks = {rev: _rw_masks(rev) for rev in sorted({c["reverse"] for c in chains})}
    strict = [masks[c["reverse"]][0] for c in chains]
    incl = [masks[c["reverse"]][1] for c in chains]
    same = [masks[c["reverse"]][2] for c in chains]
    eye = [masks[c["reverse"]][3] for c in chains]
    ids = range(n)

    def dot(a, b):
        return jnp.dot(a, b, preferred_element_type=F32)

    def dot_nt(a, b):
        return lax.dot_general(a, b, (((1,), (1,)), ((), ())), preferred_element_type=F32)

    def bd(x):
        return jnp.concatenate([x] * NQ, axis=0) * bdm_b

    def mm(ms, xs):
        return [dot(m.astype(BF16), bd(x.astype(BF16))) for m, x in zip(ms, xs)]

    lw = [c["lw"] for c in chains]
    cum = [c["cum"] for c in chains]
    tot =[cum[i][0:1] if chains[i]["reverse"] else cum[i][C - 1:C] for i in ids]
    pinv = [jnp.exp(-x) for x in cum]
    pend = [jnp.exp(t - x) for t, x in zip(tot, cum)]
    kka = [c["kk"] * c["a"] for c in chains]
    kap = [(chains[i]["kk"] * jnp.exp(cum[i] - lw[i])).astype(BF16) for i in ids]
    bet = [(kka[i] * pinv[i]).astype(BF16) for i in ids]
    kt = [(chains[i]["kd"] * pinv[i]).astype(BF16) for i in ids]
    rt = [(chains[i]["r"] * jnp.exp(cum[i])).astype(BF16) for i in ids]
    v = [c["v"] for c in chains]

    x2 = [jnp.concatenate([kap[i], rt[i]], axis=0) for i in ids]
    y2 = [jnp.concatenate([bd(bet[i]), bd(kt[i])], axis=0) for i in ids]
    aw = [dot_nt(x2[i], y2[i]) for i in ids]
    zz = [dot_nt(x2[i], chains[i]["zt"].astype(BF16)) for i in ids]
    a_ab = [jnp.where(strict[i], aw[i][:C, :NQ * C], 0.0) for i in ids]
    a_ak = [jnp.where(strict[i], aw[i][:C, NQ * C:], 0.0) for i in ids]
    a_rb = [jnp.where(incl[i], aw[i][C:, :NQ * C], 0.0) for i in ids]
    a_rk = [jnp.where(incl[i], aw[i][C:, NQ * C:], 0.0) for i in ids]
    akv = mm(a_ak, v)
    rhs = [-(zz[i][:C] + akv[i]) for i in ids]
    d = [jnp.where(same[i], a_ab[i], 0.0) for i in ids]
    e = [a_ab[i] - d[i] for i in ids]
    d2 = mm(d, d)
    d4 = mm(d2, d2)
    d8 = mm(d4, d4)
    t = [eye[i] - d[i] for i in ids]
    t = [x + y for x, y in zip(t, mm(t, d2))]
    t = [x + y for x, y in zip(t, mm(t, d4))]
    t = [x + y for x, y in zip(t, mm(t, d8))]
    nn = mm(t, e)
    n2 = mm(nn, nn)
    u = mm(t, rhs)
    u = [x + y for x, y in zip(u, mm(n2, u))]
    u = [x - y for x, y in zip(u, mm(nn, u))]
    yv = mm(a_rk, v)
    yu = mm(a_rb, u)
    y = [zz[i][C:] + yv[i] + yu[i] for i in ids]
    lhs = [jnp.concatenate([v[i], u[i]], axis=0).astype(BF16) for i in ids]
    rhs2 = [jnp.concatenate([chains[i]["kd"] * pend[i], kka[i] * pend[i]], axis=0).astype(BF16) for i in ids]
    upd = [lax.dot_general(lhs[i], rhs2[i], (((0,), (0,)), ((), ())), preferred_element_type=F32) for i in ids]
    zt = [chains[i]["zt"] * jnp.exp(tot[i]) + upd[i] * bdm for i in ids]
    return list(zip(y, zt))


RW_SEQS = 2


def _rw_scan_kernel(rf, vf, kkf, lwf, kdf, af, cf, rb, vb, kkb, lwb, kdb, ab, cb, bdm_ref, yf_ref, yb_ref, zt_ref):
    @pl.when(pl.program_id(1) == 0)
    def _():
        zt_ref[...] = jnp.zeros(zt_ref.shape, F32)

    bdm = bdm_ref[...]
    nq = RW_HEADS // RW_QUAD
    names = ("r", "lw", "kd", "v", "kk", "a", "cum")
    chains, outs = [], []
    for bi in range(RW_SEQS):
        for di, (reverse, refs, y_ref) in enumerate(((False, (rf, lwf, kdf, vf, kkf, af, cf), yf_ref),
                                                     (True, (rb, lwb, kdb, vb, kkb, ab, cb), yb_ref))):
            for q in range(nq):
                sl = slice(q * RW_QW, (q + 1) * RW_QW)
                idx = (bi * 2 + di) * nq + q
                chain = {name: ref[bi, :, sl] for name, ref in zip(names, refs)}
                chain["zt"] = zt_ref[idx]
                chain["reverse"] = reverse
                chains.append(chain)
                outs.append((y_ref, bi, sl, idx))
    for (y, zt), (y_ref, bi, sl, idx) in zip(_rw_chunks(chains, bdm, bdm.astype(BF16)), outs):
        y_ref[bi, :, sl] = y
        zt_ref[idx] = zt


def rw_scan(r, v, kk, lw, kd, a, cum, bdm):
    b, L, _ = r.shape
    assert b % RW_SEQS == 0
    nc = L // RW_CHUNK
    blk = (RW_SEQS, RW_CHUNK, RW_WIDTH)
    f1 = pl.BlockSpec(blk, lambda bi, c: (bi, c, 0))
    b1 = pl.BlockSpec(blk, lambda bi, c: (bi, nc - 1 - c, 0))
    b2 = pl.BlockSpec(blk, lambda bi, c: (bi, nc - 1 - c, 1))
    out = jax.ShapeDtypeStruct((b, L, RW_WIDTH), F32)
    return pl.pallas_call(
        _rw_scan_kernel,
        grid=(b // RW_SEQS, nc),
        in_specs=[f1, f1, f1, f1, f1, f1, f1, b1, b1, b1, b2, b2, b2, b2, _resident(bdm)],
        out_specs=[f1, b1],
        out_shape=[out, out],
        scratch_shapes=[pltpu.VMEM((RW_SEQS * 2 * RW_HEADS // RW_QUAD, RW_QW, RW_QW), F32)],
        compiler_params=_params("parallel", "arbitrary"),
        name="rw_scan",
    )(r, v, kk, lw, kd, a, cum, r, v, kk, lw, kd, a, cum, bdm)


def _rw_post_kernel(yf_ref, yb_ref, r_ref, v_ref, kd_ref, g_ref, lnw_ref, lnb_ref, rk_ref, ones_ref, o_ref):
    ones_bd = ones_ref[...]
    inv_n = 1.0 / RW_HEAD_DIM
    y = yf_ref[...] + yb_ref[...]
    yc = y - _seg_sum(y, ones_bd) * inv_n
    var = _seg_sum(yc * yc, ones_bd) * inv_n
    y = yc * lax.rsqrt(var + RW_LN_EPS) * lnw_ref[...] + lnb_ref[...]
    kd = kd_ref[:, 0:RW_WIDTH] + kd_ref[:, RW_WIDTH:]
    bonus = _seg_sum(r_ref[...] * kd * rk_ref[...], ones_bd)
    o_ref[...] = ((y + bonus * v_ref[...]) * g_ref[...]).astype(o_ref.dtype)


def rw_post(yf, yb, r, v, kd, g, lw):
    t = yf.shape[0]
    tm = _tile(t, 512)
    one = pl.BlockSpec((tm, RW_WIDTH), lambda i: (i, 0))
    two = pl.BlockSpec((tm, 2 * RW_WIDTH), lambda i: (i, 0))

    def full(a):
        return pl.BlockSpec(a.shape, lambda i: (0,) * a.ndim)

    weights = (lw["ln_w"], lw["ln_b"], lw["r_k"], lw["ones_bd"])
    return pl.pallas_call(
        _rw_post_kernel,
        grid=(t // tm,),
        in_specs=[one, one, one, one, two, one] + [full(a) for a in weights],
        out_specs=one,
        out_shape=jax.ShapeDtypeStruct((t, RW_WIDTH), BF16),
        compiler_params=_params("parallel"),
        name="rw_post",
    )(yf, yb, r, v, kd, g, *weights)


def _block_diag2(w):
    z = jnp.zeros_like(w[0])
    return jnp.concatenate([jnp.concatenate([w[0], z], axis=1), jnp.concatenate([z, w[1]], axis=1)], axis=0)


def _hi_lo(w):
    w = w.astype(F32)
    hi = w.astype(BF16)
    return hi, (w - hi.astype(F32)).astype(BF16)


def _rw_layer_weights(mu, w0, w_up, a0, a_up, g_up, k_k, k_a, r_k, ln_w, ln_b):
    wup_hi, wup_lo = _hi_lo(_block_diag2(w_up))
    aup_hi, aup_lo = _hi_lo(_block_diag2(a_up))
    gup_hi, gup_lo = _hi_lo(g_up)

    def row(t):
        return t.reshape(1, -1).astype(F32)

    return {
        "mu": row(mu), "wup_hi": wup_hi, "wup_lo": wup_lo, "w0": row(w0),
        "aup_hi": aup_hi, "aup_lo": aup_lo, "a0": row(a0), "gup_hi": gup_hi, "gup_lo": gup_lo,
        "k_k": row(k_k), "k_a": row(k_a), "r_k": row(r_k), "ln_w": row(ln_w), "ln_b": row(ln_b),
        "ones_bd": _head_ones(RW_WIDTH, RW_HEAD_DIM),
    }


def rwkv7_mix(p, lw, bdm, b, L):
    r, v, kk, g, lwd, kd, a, cum = rw_prep(p, lw, b, L)
    yf, yb = rw_scan(r, v, kk, lwd, kd, a, cum, bdm)
    t = b * L

    def flat(x):
        return x.reshape(t, x.shape[-1])

    return rw_post(flat(yf), flat(yb), flat(r), flat(v), flat(kd), flat(g), lw)


def _merge_kernel(x_ref, ya_ref, yb_ref, yc_ref, gin_ref, bg_ref, pa_ref, pb_ref, pc_ref, wo_ref, o_ref):
    mixed = None
    for i, (y_ref, p_ref) in enumerate(((ya_ref, pa_ref), (yb_ref, pb_ref), (yc_ref, pc_ref))):
        sl = slice(i * D_MODEL, (i + 1) * D_MODEL)
        gate = jax.nn.sigmoid(gin_ref[:, sl].astype(F32) + bg_ref[:, sl])
        term = gate * jnp.dot(y_ref[...], p_ref[...], preferred_element_type=F32)
        mixed = term if mixed is None else mixed + term
    o_ref[...] = x_ref[...] + jnp.dot(mixed.astype(BF16), wo_ref[...], preferred_element_type=F32)


def merge(x, ya, yb, yc, gate_in, b_gate, pa, pb, pc, wo):
    t = x.shape[0]
    tm = _tile(t, 512)

    def rows(w):
        return pl.BlockSpec((tm, w), lambda i: (i, 0))

    def full(a):
        return pl.BlockSpec(a.shape, lambda i: (0,) * a.ndim)

    return pl.pallas_call(
        _merge_kernel,
        grid=(t // tm,),
        in_specs=[rows(D_MODEL), rows(NA_WIDTH), rows(MLA_WIDTH), rows(RW_WIDTH), rows(N_BRANCH * D_MODEL),
                  full(b_gate), full(pa), full(pb), full(pc), full(wo)],
        out_specs=rows(D_MODEL),
        out_shape=jax.ShapeDtypeStruct((t, D_MODEL), F32),
        compiler_params=_params("parallel"),
        name="merge",
    )(x, ya, yb, yc, gate_in, b_gate, pa, pb, pc, wo)


def _ffn_kernel(x_ref, g_ref, wg_ref, wu_ref, wd_ref, o_ref):
    x = x_ref[...]
    ms = jnp.mean(x * x, axis=-1, keepdims=True)
    h = (x * lax.rsqrt(ms + NORM_EPS) * g_ref[...]).astype(BF16)
    gate = jnp.dot(h, wg_ref[...], preferred_element_type=F32)
    up = jnp.dot(h, wu_ref[...], preferred_element_type=F32)
    act = (gate * jax.nn.sigmoid(gate) * up).astype(BF16)
    o_ref[...] = x + jnp.dot(act, wd_ref[...], preferred_element_type=F32)


def ffn(x, g, wg, wu, wd, tm_pref=512):
    t, d = x.shape
    tm = _tile(t, tm_pref)
    g = g.reshape(1, d)
    return pl.pallas_call(
        _ffn_kernel,
        grid=(t // tm,),
        in_specs=[pl.BlockSpec((tm, d), lambda i: (i, 0))] + [_resident(a) for a in (g, wg, wu, wd)],
        out_specs=pl.BlockSpec((tm, d), lambda i: (i, 0)),
        out_shape=jax.ShapeDtypeStruct((t, d), F32),
        compiler_params=_params("parallel"),
        name="ffn",
    )(x, g, wg, wu, wd)


IN_SIZES = (NA_WIDTH, NA_WIDTH, NA_WIDTH, MLA_Q_RANK, MLA_KV_RANK, MLA_ROPE, RW_IN, N_BRANCH * D_MODEL)
IN_SPLITS = tuple(int(s) for s in np.cumsum(IN_SIZES)[:-1])


def _layer_weights(l, norm1_g, w_in, b_gate, na_q_norm, na_k_norm, na_rpb, na_proj,
                   mla_cq_norm, mla_ckv_norm, mla_w_uq, mla_w_ukv, mla_q_norm, mla_k_norm, mla_proj,
                   rw_mu, rw_w0, rw_w_up, rw_a0, rw_a_up, rw_g_up, rw_k_k, rw_k_a, rw_r_k, rw_ln_w, rw_ln_b, rw_proj,
                   w_out, norm2_g, ffn_w_gate, ffn_w_up, ffn_w_down):
    w = w_in[l]
    slab_pad = ((0, 0), (MLA_NOPE, MLA_HEAD_PAD - MLA_QK))
    w_kr = w[:, IN_SPLITS[4]:IN_SPLITS[5]]
    w_kr_rot = _rope_partner(w_kr, mla_k_norm[l].astype(F32)[MLA_NOPE:])
    w_mla = jnp.concatenate([w[:, IN_SPLITS[2]:IN_SPLITS[4]], jnp.pad(w_kr, slab_pad), jnp.pad(w_kr_rot, slab_pad)],
                            axis=1)
    return {
        "norm1_g": norm1_g[l], "w_na": w[:, :IN_SPLITS[2]].astype(BF16), "w_mla": w_mla.astype(BF16),
        "w_rw": w[:, IN_SPLITS[5]:IN_SPLITS[6]].astype(BF16),
        "w_gate": w[:, IN_SPLITS[6]:].astype(BF16),
        "b_gate": b_gate[l].reshape(1, -1).astype(F32),
        "na_gq": na_q_norm[l].astype(F32) * (NA_HEAD_DIM ** -0.5), "na_gk": na_k_norm[l].astype(F32),
        "na_bias": _na_bias_table(na_rpb[l]),
        "na_proj": na_proj[l].astype(BF16), "mla_proj": mla_proj[l].astype(BF16), "rw_proj": rw_proj[l].astype(BF16),
        "mla": _mla_layer_weights(mla_cq_norm[l], mla_ckv_norm[l], mla_w_uq[l], mla_w_ukv[l], mla_q_norm[l], mla_k_norm[l]),
        "rw": _rw_layer_weights(rw_mu[l], rw_w0[l], rw_w_up[l], rw_a0[l], rw_a_up[l], rw_g_up[l],
                                rw_k_k[l], rw_k_a[l], rw_r_k[l], rw_ln_w[l], rw_ln_b[l]),
        "w_out": w_out[l].astype(BF16), "norm2_g": norm2_g[l],
        "ffn_w_gate": ffn_w_gate[l].astype(BF16), "ffn_w_up": ffn_w_up[l].astype(BF16),
        "ffn_w_down": ffn_w_down[l].astype(BF16),
    }


def _trunk(x, layers):
    b, L, d = x.shape
    t = b * L
    x = x.reshape(t, d)
    tabs = _rope_tables(L)
    bdm = _head_ones(RW_QW, RW_HEAD_DIM).astype(F32)
    for lw in layers:
        na_in, mla_in, rw_in, gate_in = in_proj(x, lw["norm1_g"], (lw["w_na"], lw["w_mla"], lw["w_rw"], lw["w_gate"]),
                                                (BF16, BF16, F32, BF16))
        qa, ka, va = na_prep(na_in, lw["na_gq"], lw["na_gk"], b, L)
        y_a = na_attention(qa, ka, va, lw["na_bias"]).reshape(t, NA_WIDTH)
        qm, km, vm = mla_prep(mla_in, lw["mla"], tabs, b, L)
        y_b = mla_flash(qm, km, vm).reshape(t, MLA_WIDTH)
        y_c = rwkv7_mix(rw_in, lw["rw"], bdm, b, L)
        x = merge(x, y_a, y_b, y_c, gate_in, lw["b_gate"], lw["na_proj"], lw["mla_proj"], lw["rw_proj"], lw["w_out"])
        x = ffn(x, lw["norm2_g"], lw["ffn_w_gate"], lw["ffn_w_up"], lw["ffn_w_down"])
    return x.reshape(b, L, d)


def kernel(x_prompt, x_sample, norm1_g, w_in, b_gate, na_q_norm, na_k_norm, na_rpb, na_proj, mla_cq_norm, mla_ckv_norm, mla_w_uq, mla_w_ukv, mla_q_norm, mla_k_norm, mla_proj, rw_mu, rw_w0, rw_w_up, rw_a0, rw_a_up, rw_g_up, rw_k_k, rw_k_a, rw_r_k, rw_ln_w, rw_ln_b, rw_proj, w_out, norm2_g, ffn_w_gate, ffn_w_up, ffn_w_down):
    weights = (norm1_g, w_in, b_gate, na_q_norm, na_k_norm, na_rpb, na_proj,
               mla_cq_norm, mla_ckv_norm, mla_w_uq, mla_w_ukv, mla_q_norm, mla_k_norm, mla_proj,
               rw_mu, rw_w0, rw_w_up, rw_a0, rw_a_up, rw_g_up, rw_k_k, rw_k_a, rw_r_k, rw_ln_w, rw_ln_b, rw_proj,
               w_out, norm2_g, ffn_w_gate, ffn_w_up, ffn_w_down)
    layers = [_layer_weights(l, *weights) for l in range(norm1_g.shape[0])]
    return (_trunk(x_prompt, layers), _trunk(x_sample, layers))
```

```python
import functools

import jax
import jax.numpy as jnp
import numpy as np
from jax import lax
from jax.experimental import pallas as pl
from jax.experimental.pallas import tpu as pltpu

F32 = jnp.float32
BF16 = jnp.bfloat16

D_MODEL = 1024
GRID_W = 64
N_BRANCH = 3
NORM_EPS = 1e-6

NA_HEADS = 8
NA_HEAD_DIM = 64
NA_WIDTH = NA_HEADS * NA_HEAD_DIM
NA_WIN_ROWS = 8
NA_WIN_COLS = 16
NA_RPB_ROWS = 2 * NA_WIN_ROWS - 1
NA_RPB_COLS = 2 * NA_WIN_COLS - 1
NA_MASK = -1e30

MLA_HEADS = 8
MLA_NOPE = 64
MLA_ROPE = 32
MLA_QK = MLA_NOPE + MLA_ROPE
MLA_V = 64
MLA_WIDTH = MLA_HEADS * MLA_V
MLA_Q_RANK = 256
MLA_KV_RANK = 128
MLA_HEAD_PAD = 128
ROPE_THETA = 10000.0

RW_HEADS = 8
RW_HEAD_DIM = 64
RW_WIDTH = RW_HEADS * RW_HEAD_DIM
RW_DECAY_RANK = 64
RW_A_RANK = 64
RW_G_RANK = 128
RW_LN_EPS = 64e-5
RW_IN = 3 * RW_WIDTH + 2 * RW_DECAY_RANK + 2 * RW_A_RANK + RW_G_RANK

D_FF = 2816

VMEM_LIMIT_BYTES = 56 * 1024 * 1024
LANES = 128


def _params(*sem):
    return pltpu.CompilerParams(dimension_semantics=sem, vmem_limit_bytes=VMEM_LIMIT_BYTES)


def _tile(n, pref, mult=8):
    if n <= pref:
        return n
    t = (pref // mult) * mult
    while t >= mult:
        if n % t == 0:
            return t
        t -= mult
    return n


def _resident(a):
    return pl.BlockSpec(a.shape, lambda *_: (0,) * a.ndim, pipeline_mode=pl.Buffered(1))


def _in_proj_kernel(x_ref, g_ref, *refs):
    n = len(refs) // 2
    x = x_ref[...]
    ms = jnp.mean(x * x, axis=-1, keepdims=True)
    h = (x * lax.rsqrt(ms + NORM_EPS) * g_ref[...]).astype(BF16)
    for w_ref, o_ref in zip(refs[:n], refs[n:]):
        o_ref[...] = jnp.dot(h, w_ref[...], preferred_element_type=F32).astype(o_ref.dtype)


def in_proj(x, g, weights, out_dtypes, tm_pref=512):
    t, d = x.shape
    tm = _tile(t, tm_pref)
    g = g.reshape(1, d)
    return pl.pallas_call(
        _in_proj_kernel,
        grid=(t // tm,),
        in_specs=[pl.BlockSpec((tm, d), lambda i: (i, 0)), _resident(g)] + [_resident(w) for w in weights],
        out_specs=[pl.BlockSpec((tm, w.shape[1]), lambda i: (i, 0)) for w in weights],
        out_shape=[jax.ShapeDtypeStruct((t, w.shape[1]), dt) for w, dt in zip(weights, out_dtypes)],
        compiler_params=_params("parallel"),
        name="in_proj",
    )(x, g, *weights)


def _split2(x):
    hi = x.astype(BF16)
    return hi, (x - hi.astype(F32)).astype(BF16)


def _seg_sum(x, ones_bd):
    hi, lo = _split2(x)
    return jnp.dot(hi, ones_bd, preferred_element_type=F32) + jnp.dot(lo, ones_bd, preferred_element_type=F32)


def _head_ones(width, head_dim):
    head = np.arange(width) // head_dim
    return jnp.asarray(head[:, None] == head[None, :], BF16)


def _na_prep_kernel(x_ref, gq_ref, gk_ref, ones_ref, q_ref, k_ref, v_ref):
    inv_n = 1.0 / NA_HEAD_DIM

    def normed(t, g):
        return (t * lax.rsqrt(_seg_sum(t * t, ones_ref[...]) * inv_n + NORM_EPS) * g).astype(BF16)

    q = normed(x_ref[0, :, 0:NA_WIDTH].astype(F32), gq_ref[...])
    k = normed(x_ref[0, :, NA_WIDTH:2 * NA_WIDTH].astype(F32), gk_ref[...])
    for h in range(NA_HEADS):
        sl = slice(h * NA_HEAD_DIM, (h + 1) * NA_HEAD_DIM)
        q_ref[0, h] = q[:, sl]
        k_ref[0, h] = k[:, sl]
        v_ref[0, h] = x_ref[0, :, 2 * NA_WIDTH + sl.start:2 * NA_WIDTH + sl.stop]


def na_prep(na_in, gq, gk, b, L):
    tm = _tile(L, 512)
    x3 = na_in.reshape(b, L, 3 * NA_WIDTH)
    hm = jax.ShapeDtypeStruct((b, NA_HEADS, L, NA_HEAD_DIM), BF16)
    hm_spec = pl.BlockSpec((1, NA_HEADS, tm, NA_HEAD_DIM), lambda i, j: (i, 0, j, 0))
    consts = (jnp.tile(gq, NA_HEADS).reshape(1, -1), jnp.tile(gk, NA_HEADS).reshape(1, -1),
              _head_ones(NA_WIDTH, NA_HEAD_DIM))
    return pl.pallas_call(
        _na_prep_kernel,
        grid=(b, L // tm),
        in_specs=[pl.BlockSpec((1, tm, 3 * NA_WIDTH), lambda i, j: (i, j, 0))] + [_resident(a) for a in consts],
        out_specs=[hm_spec, hm_spec, hm_spec],
        out_shape=[hm, hm, hm],
        compiler_params=_params("parallel", "parallel"),
        name="na_prep",
    )(x3, *consts)


NA_ROWS_PER_STEP = 8
NA_WIN_BLOCKS = 3
NA_ROWS_PER_ITER = 4
NA_BLOCK = NA_ROWS_PER_STEP * GRID_W
NA_KEYS = NA_WIN_ROWS * GRID_W


def _na_window_block(i, nblk):
    return jnp.clip(i - 1, 0, nblk - NA_WIN_BLOCKS)


def _na_attn_kernel(q_ref, k_ref, v_ref, bias_ref, o_ref, *, rows):
    i = pl.program_id(1)
    base_row = _na_window_block(i, rows // NA_ROWS_PER_STEP) * NA_ROWS_PER_STEP

    def rows_body(j, carry):
        work = []
        for jj in range(NA_ROWS_PER_ITER):
            jr = j * NA_ROWS_PER_ITER + jj
            r = i * NA_ROWS_PER_STEP + jr
            rs = jnp.clip(r - NA_WIN_ROWS // 2, 0, rows - NA_WIN_ROWS)
            off = pl.multiple_of((rs - base_row) * GRID_W, GRID_W)
            qoff = pl.multiple_of(jr * GRID_W, GRID_W)
            work += [(h, off, r - rs, qoff) for h in range(NA_HEADS)]
        s = [lax.dot_general(q_ref[0, h, pl.ds(qoff, GRID_W), :], k_ref[0, h, pl.ds(off, NA_KEYS), :],
                             (((1,), (1,)), ((), ())), preferred_element_type=F32) + bias_ref[pat, h]
             for h, off, pat, qoff in work]
        p = [jnp.exp(x - jnp.max(x, axis=-1, keepdims=True)) for x in s]
        l = [jnp.sum(x, axis=-1, keepdims=True) for x in p]
        o = [jnp.dot(x.astype(BF16), v_ref[0, h, pl.ds(off, NA_KEYS), :], preferred_element_type=F32) / y
             for x, y, (h, off, _, _) in zip(p, l, work)]
        for x, (h, _, _, qoff) in zip(o, work):
            o_ref[0, pl.ds(qoff, GRID_W), h * NA_HEAD_DIM:(h + 1) * NA_HEAD_DIM] = x.astype(o_ref.dtype)
        return carry

    lax.fori_loop(0, NA_ROWS_PER_STEP // NA_ROWS_PER_ITER, rows_body, 0)


def _na_bias_table(rpb):
    pat = np.arange(NA_WIN_ROWS)[:, None]
    w = np.arange(NA_WIN_ROWS)[None, :]
    dr_idx = w - pat + NA_WIN_ROWS - 1
    qc = np.arange(GRID_W)[:, None]
    kc = np.arange(GRID_W)[None, :]
    win_start = np.clip(qc - NA_WIN_COLS // 2, 0, GRID_W - NA_WIN_COLS)
    ok = (kc >= win_start) & (kc < win_start + NA_WIN_COLS)
    dc_idx = np.clip(kc - qc + NA_WIN_COLS - 1, 0, NA_RPB_COLS - 1)
    t = rpb.astype(F32)[:, dr_idx][:, :, :, dc_idx]
    t = jnp.where(jnp.asarray(ok)[None, None, None], t, NA_MASK)
    t = jnp.transpose(t, (1, 0, 3, 2, 4))
    return t.reshape(NA_WIN_ROWS, NA_HEADS, GRID_W, NA_KEYS)


def na_attention(q, k, v, bias):
    b, _, L, _ = q.shape
    rows = L // GRID_W
    assert rows % NA_ROWS_PER_STEP == 0
    nblk = rows // NA_ROWS_PER_STEP
    assert nblk >= NA_WIN_BLOCKS
    cur = pl.BlockSpec((1, NA_HEADS, NA_BLOCK, NA_HEAD_DIM), lambda bi, i: (bi, 0, i, 0))
    window = pl.BlockSpec((pl.Element(1), pl.Element(NA_HEADS), pl.Element(NA_WIN_BLOCKS * NA_BLOCK),
                           pl.Element(NA_HEAD_DIM)),
                          lambda bi, i: (bi, 0, _na_window_block(i, nblk) * NA_BLOCK, 0))
    return pl.pallas_call(
        functools.partial(_na_attn_kernel, rows=rows),
        grid=(b, nblk),
        in_specs=[cur, window, window, _resident(bias)],
        out_specs=pl.BlockSpec((1, NA_BLOCK, NA_WIDTH), lambda bi, i: (bi, i, 0)),
        out_shape=jax.ShapeDtypeStruct((b, L, NA_WIDTH), BF16),
        compiler_params=_params("parallel", "parallel"),
        name="na_attn",
    )(q, k, v, bias)


def _rope_tables(L):
    t = np.arange(L)
    row = (t // GRID_W).astype(np.float32)
    col = (t % GRID_W).astype(np.float32)
    n_freq = MLA_ROPE // 4
    inv_freq = jnp.asarray(ROPE_THETA, F32) ** (-jnp.arange(n_freq, dtype=F32) / n_freq)
    ang = jnp.concatenate([jnp.asarray(row)[:, None] * inv_freq, jnp.asarray(col)[:, None] * inv_freq], axis=-1)
    cos, sin = jnp.cos(ang), jnp.sin(ang)
    ones = jnp.ones((L, MLA_NOPE), F32)
    zpad = jnp.zeros((L, MLA_HEAD_PAD - MLA_QK), F32)
    znope = jnp.zeros((L, MLA_NOPE), F32)
    c_tab = jnp.concatenate([ones, cos, cos, zpad], axis=-1)
    s_tab = jnp.concatenate([znope, sin, sin, zpad], axis=-1)
    return c_tab, s_tab


def _mla_prep_kernel(x_ref, gcq_ref, gckv_ref, wuq_ref, wuqr_ref, wuk_ref, wuv_ref, vones_ref, gq_ref, gk_ref,
                     c_ref, s_ref, q_ref, k_ref, v_ref):
    lo = MLA_Q_RANK + MLA_KV_RANK
    cq = x_ref[0, :, 0:MLA_Q_RANK].astype(F32)
    ckv = x_ref[0, :, MLA_Q_RANK:lo].astype(F32)
    kr = x_ref[0, :, lo:lo + MLA_HEAD_PAD].astype(F32)
    kr_rot = x_ref[0, :, lo + MLA_HEAD_PAD:].astype(F32)
    cqn = (cq * lax.rsqrt(jnp.mean(cq * cq, axis=-1, keepdims=True) + NORM_EPS) * gcq_ref[...]).astype(BF16)
    ckvn = (ckv * lax.rsqrt(jnp.mean(ckv * ckv, axis=-1, keepdims=True) + NORM_EPS) * gckv_ref[...]).astype(BF16)
    q_raw = jnp.dot(cqn, wuq_ref[...], preferred_element_type=F32)
    q_rot = jnp.dot(cqn, wuqr_ref[...], preferred_element_type=F32)
    k_raw = jnp.dot(ckvn, wuk_ref[...], preferred_element_type=F32)
    v_ref[0] = (jnp.dot(ckvn, wuv_ref[...], preferred_element_type=F32) + vones_ref[...]).astype(BF16)
    s = s_ref[...]
    gqc = gq_ref[...] * c_ref[...]
    gkc = gk_ref[...] * c_ref[...]
    kr_rot_s = kr_rot * s

    def norm_rope(t, gc, rot_s):
        inv = lax.rsqrt(jnp.sum(t * t, axis=-1, keepdims=True) * (1.0 / MLA_QK) + NORM_EPS)
        return (inv * (t * gc + rot_s)).astype(BF16)

    for h in range(MLA_HEADS):
        sl = slice(h * MLA_HEAD_PAD, (h + 1) * MLA_HEAD_PAD)
        q_ref[0, :, sl] = norm_rope(q_raw[:, sl], gqc, q_rot[:, sl] * s)
        k_ref[0, :, sl] = norm_rope(k_raw[:, sl] + kr, gkc, kr_rot_s)


def mla_prep(mla_in, lw, tabs, b, L):
    tm = _tile(L, 512)
    width = MLA_HEADS * MLA_HEAD_PAD
    in_w = mla_in.shape[-1]
    x3 = mla_in.reshape(b, L, in_w)
    tab_spec = pl.BlockSpec((tm, MLA_HEAD_PAD), lambda i, j: (j, 0))
    weights = (lw["gcq"], lw["gckv"], lw["wuq"], lw["wuq_rot"], lw["wuk"], lw["wuv"], lw["v_ones"], lw["gq"], lw["gk"])
    slab_spec = pl.BlockSpec((1, tm, width), lambda i, j: (i, j, 0))
    slab = jax.ShapeDtypeStruct((b, L, width), BF16)
    return pl.pallas_call(
        _mla_prep_kernel,
        grid=(b, L // tm),
        in_specs=[pl.BlockSpec((1, tm, in_w), lambda i, j: (i, j, 0))]
        + [_resident(a) for a in weights] + [tab_spec] * 2,
        out_specs=[slab_spec, slab_spec, slab_spec],
        out_shape=[slab, slab, slab],
        compiler_params=_params("parallel", "parallel"),
        name="mla_prep",
    )(x3, *weights, *tabs)


def _mla_flash_kernel(q_ref, k_ref, v_ref, o_ref, m_ref, acc_ref):
    ki = pl.program_id(2)
    tk = k_ref.shape[1]

    @pl.when(ki == 0)
    def _():
        m_ref[...] = jnp.full(m_ref.shape, -jnp.inf, F32)
        acc_ref[...] = jnp.zeros(acc_ref.shape, F32)

    def scores(h):
        sl = slice(h * MLA_HEAD_PAD, (h + 1) * MLA_HEAD_PAD)
        return lax.dot_general(q_ref[0, :, sl], k_ref[0, :, sl], (((1,), (1,)), ((), ())),
                               preferred_element_type=F32)

    s_next = scores(0)
    for h in range(MLA_HEADS):
        sl = slice(h * MLA_HEAD_PAD, (h + 1) * MLA_HEAD_PAD)
        s = s_next
        if h + 1 < MLA_HEADS:
            s_next = scores(h + 1)
        m_prev = m_ref[h]
        m_next = jnp.maximum(m_prev, jnp.max(s, axis=-1, keepdims=True))
        alpha = jnp.exp2(m_prev - m_next)
        p = jnp.exp2(s - jnp.concatenate([m_next] * (tk // LANES), axis=1))
        acc_ref[h] = alpha * acc_ref[h] + jnp.dot(p.astype(BF16), v_ref[0, :, sl], preferred_element_type=F32)
        m_ref[h] = m_next

    @pl.when(ki == pl.num_programs(2) - 1)
    def _():
        lane = lax.broadcasted_iota(jnp.int32, (acc_ref.shape[1], LANES), 1)

        def normalised(h):
            acc = acc_ref[h]
            return acc / pltpu.roll(acc, MLA_V, 1)

        for hp in range(MLA_HEADS // 2):
            pair = jnp.where(lane < MLA_V, normalised(2 * hp), pltpu.roll(normalised(2 * hp + 1), MLA_V, 1))
            o_ref[0, :, hp * LANES:(hp + 1) * LANES] = pair.astype(o_ref.dtype)


def mla_flash(q, k, v, tq_pref=512, tk_pref=2048):
    b, L, width = q.shape
    tq = _tile(L, tq_pref)
    tk = _tile(L, tk_pref, LANES)
    return pl.pallas_call(
        _mla_flash_kernel,
        grid=(b, L // tq, L // tk),
        in_specs=[pl.BlockSpec((1, tq, width), lambda bi, qi, ki: (bi, qi, 0)),
                  pl.BlockSpec((1, tk, width), lambda bi, qi, ki: (bi, ki, 0)),
                  pl.BlockSpec((1, tk, width), lambda bi, qi, ki: (bi, ki, 0))],
        out_specs=pl.BlockSpec((1, tq, MLA_WIDTH), lambda bi, qi, ki: (bi, qi, 0)),
        out_shape=jax.ShapeDtypeStruct((b, L, MLA_WIDTH), BF16),
        scratch_shapes=[pltpu.VMEM((MLA_HEADS, tq, LANES), F32),
                        pltpu.VMEM((MLA_HEADS, tq, LANES), F32)],
        compiler_params=_params("parallel", "parallel", "arbitrary"),
        name="mla_flash",
    )(q, k, v)


def _rope_partner(w, g):
    half = MLA_ROPE // 2
    wg = w.astype(F32) * g
    return jnp.concatenate([-wg[..., half:], wg[..., :half]], axis=-1)


def _mla_layer_weights(cq_norm, ckv_norm, w_uq, w_ukv, q_norm, k_norm):
    pad = MLA_HEAD_PAD - MLA_QK
    gq = q_norm.astype(F32) * (MLA_QK ** -0.5 * np.log2(np.e))
    wuq3 = w_uq.reshape(MLA_Q_RANK, MLA_HEADS, MLA_QK)
    wuq = jnp.pad(wuq3, ((0, 0), (0, 0), (0, pad)))
    wuq_rot = jnp.pad(_rope_partner(wuq3[:, :, MLA_NOPE:], gq[MLA_NOPE:]), ((0, 0), (0, 0), (MLA_NOPE, pad)))
    wukv = w_ukv.reshape(MLA_KV_RANK, MLA_HEADS, MLA_NOPE + MLA_V)
    wuk = jnp.pad(wukv[:, :, :MLA_NOPE], ((0, 0), (0, 0), (0, MLA_HEAD_PAD - MLA_NOPE)))
    wuv = jnp.pad(wukv[:, :, MLA_NOPE:], ((0, 0), (0, 0), (0, MLA_HEAD_PAD - MLA_V)))
    v_ones = np.tile(np.arange(MLA_HEAD_PAD) >= MLA_V, MLA_HEADS).astype(np.float32)
    return {
        "v_ones": jnp.asarray(v_ones).reshape(1, -1),
        "gcq": cq_norm.reshape(1, -1).astype(F32),
        "gckv": ckv_norm.reshape(1, -1).astype(F32),
        "wuq": wuq.reshape(MLA_Q_RANK, -1).astype(BF16),
        "wuq_rot": wuq_rot.reshape(MLA_Q_RANK, -1).astype(BF16),
        "wuk": wuk.reshape(MLA_KV_RANK, -1).astype(BF16),
        "wuv": wuv.reshape(MLA_KV_RANK, -1).astype(BF16),
        "gq": jnp.pad(gq, (0, pad)).reshape(1, -1),
        "gk": jnp.pad(k_norm.astype(F32), (0, pad)).reshape(1, -1),
    }


RW_CHUNK = 64
RW_SUB = 16
RW_QUAD = 4
RW_QW = RW_QUAD * RW_HEAD_DIM
RW_SPLITS = (RW_WIDTH, 2 * RW_WIDTH, 3 * RW_WIDTH, 3 * RW_WIDTH + 2 * RW_DECAY_RANK,
             3 * RW_WIDTH + 2 * RW_DECAY_RANK + 2 * RW_A_RANK)
HALO = 8


def _dot3(x, w_hi, w_lo):
    hi, lo = _split2(x)
    return (jnp.dot(hi, w_hi, preferred_element_type=F32) + jnp.dot(lo, w_hi, preferred_element_type=F32)
            + jnp.dot(hi, w_lo, preferred_element_type=F32))


def _chunk_cumsum(x, tri):
    hi = x.astype(BF16)
    r1 = x - hi.astype(F32)
    mid = r1.astype(BF16)
    lo = (r1 - mid.astype(F32)).astype(BF16)
    return (jnp.dot(tri, hi, preferred_element_type=F32) + jnp.dot(tri, mid, preferred_element_type=F32)
            + jnp.dot(tri, lo, preferred_element_type=F32))


def _rw_prep_kernel(p_ref, pp_ref, pn_ref, mu_ref, wup_hi, wup_lo, w0_ref, aup_hi, aup_lo, a0_ref,
                    gup_hi, gup_lo, kk_ref, ka_ref, ones_ref, trif_ref, trib_ref,
                    r_ref, v_ref, kkn_ref, g_ref, lw_ref, kd_ref, a_ref, cum_ref):
    j = pl.program_id(1)
    p = p_ref[0]
    tm = p.shape[0]
    prev_row = jnp.where(j > 0, pp_ref[0, HALO - 1:HALO, :], 0.0)
    next_row = jnp.where(j < pl.num_programs(1) - 1, pn_ref[0, 0:1, :], 0.0)
    row = lax.broadcasted_iota(jnp.int32, p.shape, 0)
    prev = jnp.where(row == 0, prev_row, pltpu.roll(p, 1, 0))
    nxt = jnp.where(row == tm - 1, next_row, pltpu.roll(p, tm - 1, 0))
    pm = p + mu_ref[...] * (0.5 * (prev + nxt) - p)
    s0, s1, s2, s3, s4 = RW_SPLITS
    r, k, v = pm[:, 0:s0], pm[:, s0:s1], pm[:, s1:s2]
    wd = jnp.tanh(pm[:, s2:s3])
    ad = pm[:, s3:s4]
    gd = pm[:, s4:]
    w_raw = w0_ref[...] + _dot3(wd, wup_hi[...], wup_lo[...])
    z = -w_raw
    softplus = jnp.maximum(z, 0.0) + jnp.log(1.0 + jnp.exp(-jnp.abs(z)))
    lw = -jnp.exp(-softplus - 0.5)
    a = jax.nn.sigmoid(a0_ref[...] + _dot3(ad, aup_hi[...], aup_lo[...]))
    g_ref[0] = _dot3(jax.nn.sigmoid(gd), gup_hi[...], gup_lo[...])
    kk = k * kk_ref[...]
    kkn_ref[0] = kk * lax.rsqrt(_seg_sum(kk * kk, ones_ref[...]) + 1e-12)
    for d, tri_ref in enumerate((trif_ref, trib_ref)):
        sl = slice(d * RW_WIDTH, (d + 1) * RW_WIDTH)
        lw_ref[d, 0] = lw[:, sl]
        cum_ref[d, 0] = _chunk_cumsum(lw[:, sl], tri_ref[...])
        a_ref[d, 0] = a[:, sl]
        kd_ref[d, 0] = k * (1.0 + (a[:, sl] - 1.0) * ka_ref[...])
    r_ref[0] = r
    v_ref[0] = v


def rw_prep(p, lw, b, L):
    tm = _tile(L, 256, RW_CHUNK)
    nh = L // HALO
    p3 = p.reshape(b, L, RW_IN)
    t_i = np.arange(tm)[:, None]
    s_i = np.arange(tm)[None, :]
    same_chunk = (t_i // RW_CHUNK) == (s_i // RW_CHUNK)
    tri_f = jnp.asarray(same_chunk & (s_i <= t_i), BF16)
    tri_b = jnp.asarray(same_chunk & (s_i >= t_i), BF16)
    weights = (lw["mu"], lw["wup_hi"], lw["wup_lo"], lw["w0"], lw["aup_hi"], lw["aup_lo"], lw["a0"],
               lw["gup_hi"], lw["gup_lo"], lw["k_k"], lw["k_a"], lw["ones_bd"], tri_f, tri_b)
    one = jax.ShapeDtypeStruct((b, L, RW_WIDTH), F32)
    two = jax.ShapeDtypeStruct((2, b, L, RW_WIDTH), F32)
    one_spec = pl.BlockSpec((1, tm, RW_WIDTH), lambda i, j: (i, j, 0))
    two_spec = pl.BlockSpec((2, 1, tm, RW_WIDTH), lambda i, j: (0, i, j, 0))
    return pl.pallas_call(
        _rw_prep_kernel,
        grid=(b, L // tm),
        in_specs=[pl.BlockSpec((1, tm, RW_IN), lambda i, j: (i, j, 0)),
                  pl.BlockSpec((1, HALO, RW_IN), lambda i, j: (i, jnp.maximum(j * (tm // HALO) - 1, 0), 0)),
                  pl.BlockSpec((1, HALO, RW_IN), lambda i, j: (i, jnp.minimum((j + 1) * (tm // HALO), nh - 1), 0))]
        + [_resident(a) for a in weights],
        out_specs=[one_spec, one_spec, one_spec, one_spec, two_spec, two_spec, two_spec, two_spec],
        out_shape=[one, one, one, one, two, two, two, two],
        compiler_params=_params("parallel", "parallel"),
        name="rw_prep",
    )(p3, p3, p3, *weights)


def _rw_masks(reverse):
    C, NQ = RW_CHUNK, RW_QUAD
    wt = lax.broadcasted_iota(jnp.int32, (C, NQ * C), 0)
    ws = lax.broadcasted_iota(jnp.int32, (C, NQ * C), 1) & (C - 1)
    strict = (ws > wt) if reverse else (ws < wt)
    incl = (ws >= wt) if reverse else (ws <= wt)
    same = (ws // RW_SUB) == (wt // RW_SUB)
    eye = jnp.where(ws == wt, 1.0, 0.0)
    return strict, incl, same, eye


def _rw_chunks(chains, bdm, bdm_b):
    C, NQ = RW_CHUNK, RW_QUAD
    n = len(chains)
    masks = {rev: _rw_masks(rev) for rev in sorted({c["reverse"] for c in chains})}
    strict = [masks[c["reverse"]][0] for c in chains]
    incl = [masks[c["reverse"]][1] for c in chains]
    same = [masks[c["reverse"]][2] for c in chains]
    eye = [masks[c["reverse"]][3] for c in chains]
    ids = range(n)

    def dot(a, b):
        return jnp.dot(a, b, preferred_element_type=F32)

    def dot_nt(a, b):
        return lax.dot_general(a, b, (((1,), (1,)), ((), ())), preferred_element_type=F32)

    def bd(x):
        return jnp.concatenate([x] * NQ, axis=0) * bdm_b

    def mm(ms, xs):
        return [dot(m.astype(BF16), bd(x.astype(BF16))) for m, x in zip(ms, xs)]

    lw = [c["lw"] for c in chains]
    cum = [c["cum"] for c in chains]
    tot =[cum[i][0:1] if chains[i]["reverse"] else cum[i][C - 1:C] for i in ids]
    pinv = [jnp.exp(-x) for x in cum]
    pend = [jnp.exp(t - x) for t, x in zip(tot, cum)]
    kka = [c["kk"] * c["a"] for c in chains]
    kap = [(chains[i]["kk"] * jnp.exp(cum[i] - lw[i])).astype(BF16) for i in ids]
    bet = [(kka[i] * pinv[i]).astype(BF16) for i in ids]
    kt = [(chains[i]["kd"] * pinv[i]).astype(BF16) for i in ids]
    rt = [(chains[i]["r"] * jnp.exp(cum[i])).astype(BF16) for i in ids]
    v = [c["v"] for c in chains]

    x2 = [jnp.concatenate([kap[i], rt[i]], axis=0) for i in ids]
    y2 = [jnp.concatenate([bd(bet[i]), bd(kt[i])], axis=0) for i in ids]
    aw = [dot_nt(x2[i], y2[i]) for i in ids]
    zz = [dot_nt(x2[i], chains[i]["zt"].astype(BF16)) for i in ids]
    a_ab = [jnp.where(strict[i], aw[i][:C, :NQ * C], 0.0) for i in ids]
    a_ak = [jnp.where(strict[i], aw[i][:C, NQ * C:], 0.0) for i in ids]
    a_rb = [jnp.where(incl[i], aw[i][C:, :NQ * C], 0.0) for i in ids]
    a_rk = [jnp.where(incl[i], aw[i][C:, NQ * C:], 0.0) for i in ids]
    def mm2(tops, bottoms, xs):
        both = mm([jnp.concatenate([a, b], axis=0) for a, b in zip(tops, bottoms)], xs)
        return [x[:C] for x in both], [x[C:] for x in both]

    akv, yv = mm2(a_ak, a_rk, v)
    rhs = [-(zz[i][:C] + akv[i]) for i in ids]
    d = [jnp.where(same[i], a_ab[i], 0.0) for i in ids]
    e = [a_ab[i] - d[i] for i in ids]
    d2 = mm(d, d)
    t = [eye[i] - d[i] for i in ids]
    d4, td = mm2(d2, t, d2)
    t = [x + y for x, y in zip(t, td)]
    d8, td = mm2(d4, t, d4)
    t = [x + y for x, y in zip(t, td)]
    t = [x + y for x, y in zip(t, mm(t, d8))]
    nn = mm(t, e)
    n2 = mm(nn, nn)
    u = mm(t, rhs)
    u = [x + y for x, y in zip(u, mm(n2, u))]
    u = [x - y for x, y in zip(u, mm(nn, u))]
    yu = mm(a_rb, u)
    y = [zz[i][C:] + yv[i] + yu[i] for i in ids]
    lhs = [jnp.concatenate([v[i], u[i]], axis=0).astype(BF16) for i in ids]
    rhs2 = [jnp.concatenate([chains[i]["kd"] * pend[i], kka[i] * pend[i]], axis=0).astype(BF16) for i in ids]
    upd = [lax.dot_general(lhs[i], rhs2[i], (((0,), (0,)), ((), ())), preferred_element_type=F32) for i in ids]
    zt = [chains[i]["zt"] * jnp.exp(tot[i]) + upd[i] * bdm for i in ids]
    return list(zip(y, zt))


RW_SEQS = 4


def _rw_scan_kernel(rf, vf, kkf, lwf, kdf, af, cf, rb, vb, kkb, lwb, kdb, ab, cb, bdm_ref, yf_ref, yb_ref, zt_ref):
    @pl.when(pl.program_id(1) == 0)
    def _():
        zt_ref[...] = jnp.zeros(zt_ref.shape, F32)

    bdm = bdm_ref[...]
    nq = RW_HEADS // RW_QUAD
    names = ("r", "lw", "kd", "v", "kk", "a", "cum")
    chains, outs = [], []
    for bi in range(RW_SEQS):
        for di, (reverse, refs, y_ref) in enumerate(((False, (rf, lwf, kdf, vf, kkf, af, cf), yf_ref),
                                                     (True, (rb, lwb, kdb, vb, kkb, ab, cb), yb_ref))):
            for q in range(nq):
                sl = slice(q * RW_QW, (q + 1) * RW_QW)
                idx = (bi * 2 + di) * nq + q
                chain = {name: ref[bi, :, sl] for name, ref in zip(names, refs)}
                chain["zt"] = zt_ref[idx]
                chain["reverse"] = reverse
                chains.append(chain)
                outs.append((y_ref, bi, sl, idx))
    for (y, zt), (y_ref, bi, sl, idx) in zip(_rw_chunks(chains, bdm, bdm.astype(BF16)), outs):
        y_ref[bi, :, sl] = y
        zt_ref[idx] = zt


def rw_scan(r, v, kk, lw, kd, a, cum, bdm):
    b, L, _ = r.shape
    assert b % RW_SEQS == 0
    nc = L // RW_CHUNK
    blk = (RW_SEQS, RW_CHUNK, RW_WIDTH)
    f1 = pl.BlockSpec(blk, lambda bi, c: (bi, c, 0))
    b1 = pl.BlockSpec(blk, lambda bi, c: (bi, nc - 1 - c, 0))
    f2 = pl.BlockSpec((None,) + blk, lambda bi, c: (0, bi, c, 0))
    b2 = pl.BlockSpec((None,) + blk, lambda bi, c: (1, bi, nc - 1 - c, 0))
    out = jax.ShapeDtypeStruct((b, L, RW_WIDTH), F32)
    return pl.pallas_call(
        _rw_scan_kernel,
        grid=(b // RW_SEQS, nc),
        in_specs=[f1, f1, f1, f2, f2, f2, f2, b1, b1, b1, b2, b2, b2, b2, _resident(bdm)],
        out_specs=[f1, b1],
        out_shape=[out, out],
        scratch_shapes=[pltpu.VMEM((RW_SEQS * 2 * RW_HEADS // RW_QUAD, RW_QW, RW_QW), F32)],
        compiler_params=_params("parallel", "arbitrary"),
        name="rw_scan",
    )(r, v, kk, lw, kd, a, cum, r, v, kk, lw, kd, a, cum, bdm)


def _rw_post_kernel(yf_ref, yb_ref, r_ref, v_ref, kdf_ref, kdb_ref, g_ref, lnw_ref, lnb_ref, rk_ref, ones_ref, o_ref):
    ones_bd = ones_ref[...]
    inv_n = 1.0 / RW_HEAD_DIM
    y = yf_ref[...] + yb_ref[...]
    yc = y - _seg_sum(y, ones_bd) * inv_n
    var = _seg_sum(yc * yc, ones_bd) * inv_n
    y = yc * lax.rsqrt(var + RW_LN_EPS) * lnw_ref[...] + lnb_ref[...]
    kd = kdf_ref[...] + kdb_ref[...]
    bonus = _seg_sum(r_ref[...] * kd * rk_ref[...], ones_bd)
    o_ref[...] = ((y + bonus * v_ref[...]) * g_ref[...]).astype(o_ref.dtype)


def rw_post(yf, yb, r, v, kd, g, lw):
    t = yf.shape[0]
    tm = _tile(t, 512)
    one = pl.BlockSpec((tm, RW_WIDTH), lambda i: (i, 0))
    kd_f = pl.BlockSpec((None, tm, RW_WIDTH), lambda i: (0, i, 0))
    kd_b = pl.BlockSpec((None, tm, RW_WIDTH), lambda i: (1, i, 0))
    weights = (lw["ln_w"], lw["ln_b"], lw["r_k"], lw["ones_bd"])
    return pl.pallas_call(
        _rw_post_kernel,
        grid=(t // tm,),
        in_specs=[one, one, one, one, kd_f, kd_b, one] + [_resident(a) for a in weights],
        out_specs=one,
        out_shape=jax.ShapeDtypeStruct((t, RW_WIDTH), BF16),
        compiler_params=_params("parallel"),
        name="rw_post",
    )(yf, yb, r, v, kd, kd, g, *weights)


def _block_diag2(w):
    z = jnp.zeros_like(w[0])
    return jnp.concatenate([jnp.concatenate([w[0], z], axis=1), jnp.concatenate([z, w[1]], axis=1)], axis=0)


def _hi_lo(w):
    w = w.astype(F32)
    hi = w.astype(BF16)
    return hi, (w - hi.astype(F32)).astype(BF16)


def _rw_layer_weights(mu, w0, w_up, a0, a_up, g_up, k_k, k_a, r_k, ln_w, ln_b):
    wup_hi, wup_lo = _hi_lo(_block_diag2(w_up))
    aup_hi, aup_lo = _hi_lo(_block_diag2(a_up))
    gup_hi, gup_lo = _hi_lo(g_up)

    def row(t):
        return t.reshape(1, -1).astype(F32)

    return {
        "mu": row(mu), "wup_hi": wup_hi, "wup_lo": wup_lo, "w0": row(w0),
        "aup_hi": aup_hi, "aup_lo": aup_lo, "a0": row(a0), "gup_hi": gup_hi, "gup_lo": gup_lo,
        "k_k": row(k_k), "k_a": row(k_a), "r_k": row(r_k), "ln_w": row(ln_w), "ln_b": row(ln_b),
        "ones_bd": _head_ones(RW_WIDTH, RW_HEAD_DIM),
    }


def rwkv7_mix(p, lw, bdm, b, L):
    r, v, kk, g, lwd, kd, a, cum = rw_prep(p, lw, b, L)
    yf, yb = rw_scan(r, v, kk, lwd, kd, a, cum, bdm)
    t = b * L

    def flat(x):
        return x.reshape(t, x.shape[-1])

    return rw_post(flat(yf), flat(yb), flat(r), flat(v), kd.reshape(2, t, RW_WIDTH), flat(g), lw)


def _merge_kernel(x_ref, ya_ref, yb_ref, yc_ref, gin_ref, bg_ref, pa_ref, pb_ref, pc_ref, wo_ref, o_ref):
    mixed = None
    for i, (y_ref, p_ref) in enumerate(((ya_ref, pa_ref), (yb_ref, pb_ref), (yc_ref, pc_ref))):
        sl = slice(i * D_MODEL, (i + 1) * D_MODEL)
        gate = jax.nn.sigmoid(gin_ref[:, sl].astype(F32) + bg_ref[:, sl])
        term = gate * jnp.dot(y_ref[...], p_ref[...], preferred_element_type=F32)
        mixed = term if mixed is None else mixed + term
    o_ref[...] = x_ref[...] + jnp.dot(mixed.astype(BF16), wo_ref[...], preferred_element_type=F32)


def merge(x, ya, yb, yc, gate_in, b_gate, pa, pb, pc, wo):
    t = x.shape[0]
    tm = _tile(t, 512)

    def rows(w):
        return pl.BlockSpec((tm, w), lambda i: (i, 0))

    def full(a):
        return pl.BlockSpec(a.shape, lambda i: (0,) * a.ndim)

    return pl.pallas_call(
        _merge_kernel,
        grid=(t // tm,),
        in_specs=[rows(D_MODEL), rows(NA_WIDTH), rows(MLA_WIDTH), rows(RW_WIDTH), rows(N_BRANCH * D_MODEL),
                  full(b_gate), full(pa), full(pb), full(pc), full(wo)],
        out_specs=rows(D_MODEL),
        out_shape=jax.ShapeDtypeStruct((t, D_MODEL), F32),
        compiler_params=_params("parallel"),
        name="merge",
    )(x, ya, yb, yc, gate_in, b_gate, pa, pb, pc, wo)


def _ffn_kernel(x_ref, g_ref, wg_ref, wu_ref, wd_ref, o_ref):
    x = x_ref[...]
    ms = jnp.mean(x * x, axis=-1, keepdims=True)
    h = (x * lax.rsqrt(ms + NORM_EPS) * g_ref[...]).astype(BF16)
    gate = jnp.dot(h, wg_ref[...], preferred_element_type=F32)
    up = jnp.dot(h, wu_ref[...], preferred_element_type=F32)
    act = (gate * jax.nn.sigmoid(gate) * up).astype(BF16)
    o_ref[...] = x + jnp.dot(act, wd_ref[...], preferred_element_type=F32)


def ffn(x, g, wg, wu, wd, tm_pref=512):
    t, d = x.shape
    tm = _tile(t, tm_pref)
    g = g.reshape(1, d)
    return pl.pallas_call(
        _ffn_kernel,
        grid=(t // tm,),
        in_specs=[pl.BlockSpec((tm, d), lambda i: (i, 0))] + [_resident(a) for a in (g, wg, wu, wd)],
        out_specs=pl.BlockSpec((tm, d), lambda i: (i, 0)),
        out_shape=jax.ShapeDtypeStruct((t, d), F32),
        compiler_params=_params("parallel"),
        name="ffn",
    )(x, g, wg, wu, wd)


IN_SIZES = (NA_WIDTH, NA_WIDTH, NA_WIDTH, MLA_Q_RANK, MLA_KV_RANK, MLA_ROPE, RW_IN, N_BRANCH * D_MODEL)
IN_SPLITS = tuple(int(s) for s in np.cumsum(IN_SIZES)[:-1])


def _layer_weights(l, norm1_g, w_in, b_gate, na_q_norm, na_k_norm, na_rpb, na_proj,
                   mla_cq_norm, mla_ckv_norm, mla_w_uq, mla_w_ukv, mla_q_norm, mla_k_norm, mla_proj,
                   rw_mu, rw_w0, rw_w_up, rw_a0, rw_a_up, rw_g_up, rw_k_k, rw_k_a, rw_r_k, rw_ln_w, rw_ln_b, rw_proj,
                   w_out, norm2_g, ffn_w_gate, ffn_w_up, ffn_w_down):
    w = w_in[l]
    slab_pad = ((0, 0), (MLA_NOPE, MLA_HEAD_PAD - MLA_QK))
    w_kr = w[:, IN_SPLITS[4]:IN_SPLITS[5]]
    w_kr_rot = _rope_partner(w_kr, mla_k_norm[l].astype(F32)[MLA_NOPE:])
    w_mla = jnp.concatenate([w[:, IN_SPLITS[2]:IN_SPLITS[4]], jnp.pad(w_kr, slab_pad), jnp.pad(w_kr_rot, slab_pad)],
                            axis=1)
    return {
        "norm1_g": norm1_g[l], "w_na": w[:, :IN_SPLITS[2]].astype(BF16), "w_mla": w_mla.astype(BF16),
        "w_rw": w[:, IN_SPLITS[5]:IN_SPLITS[6]].astype(BF16),
        "w_gate": w[:, IN_SPLITS[6]:].astype(BF16),
        "b_gate": b_gate[l].reshape(1, -1).astype(F32),
        "na_gq": na_q_norm[l].astype(F32) * (NA_HEAD_DIM ** -0.5), "na_gk": na_k_norm[l].astype(F32),
        "na_bias": _na_bias_table(na_rpb[l]),
        "na_proj": na_proj[l].astype(BF16), "mla_proj": mla_proj[l].astype(BF16), "rw_proj": rw_proj[l].astype(BF16),
        "mla": _mla_layer_weights(mla_cq_norm[l], mla_ckv_norm[l], mla_w_uq[l], mla_w_ukv[l], mla_q_norm[l], mla_k_norm[l]),
        "rw": _rw_layer_weights(rw_mu[l], rw_w0[l], rw_w_up[l], rw_a0[l], rw_a_up[l], rw_g_up[l],
                                rw_k_k[l], rw_k_a[l], rw_r_k[l], rw_ln_w[l], rw_ln_b[l]),
        "w_out": w_out[l].astype(BF16), "norm2_g": norm2_g[l],
        "ffn_w_gate": ffn_w_gate[l].astype(BF16), "ffn_w_up": ffn_w_up[l].astype(BF16),
        "ffn_w_down": ffn_w_down[l].astype(BF16),
    }


def _trunk(x, layers):
    b, L, d = x.shape
    t = b * L
    x = x.reshape(t, d)
    tabs = _rope_tables(L)
    bdm = _head_ones(RW_QW, RW_HEAD_DIM).astype(F32)
    for lw in layers:
        na_in, mla_in, rw_in, gate_in = in_proj(x, lw["norm1_g"], (lw["w_na"], lw["w_mla"], lw["w_rw"], lw["w_gate"]),
                                                (BF16, BF16, F32, BF16))
        qa, ka, va = na_prep(na_in, lw["na_gq"], lw["na_gk"], b, L)
        y_a = na_attention(qa, ka, va, lw["na_bias"]).reshape(t, NA_WIDTH)
        qm, km, vm = mla_prep(mla_in, lw["mla"], tabs, b, L)
        y_b = mla_flash(qm, km, vm).reshape(t, MLA_WIDTH)
        y_c = rwkv7_mix(rw_in, lw["rw"], bdm, b, L)
        x = merge(x, y_a, y_b, y_c, gate_in, lw["b_gate"], lw["na_proj"], lw["mla_proj"], lw["rw_proj"], lw["w_out"])
        x = ffn(x, lw["norm2_g"], lw["ffn_w_gate"], lw["ffn_w_up"], lw["ffn_w_down"])
    return x.reshape(b, L, d)


def kernel(x_prompt, x_sample, norm1_g, w_in, b_gate, na_q_norm, na_k_norm, na_rpb, na_proj, mla_cq_norm, mla_ckv_norm, mla_w_uq, mla_w_ukv, mla_q_norm, mla_k_norm, mla_proj, rw_mu, rw_w0, rw_w_up, rw_a0, rw_a_up, rw_g_up, rw_k_k, rw_k_a, rw_r_k, rw_ln_w, rw_ln_b, rw_proj, w_out, norm2_g, ffn_w_gate, ffn_w_up, ffn_w_down):
    weights = (norm1_g, w_in, b_gate, na_q_norm, na_k_norm, na_rpb, na_proj,
               mla_cq_norm, mla_ckv_norm, mla_w_uq, mla_w_ukv, mla_q_norm, mla_k_norm, mla_proj,
               rw_mu, rw_w0, rw_w_up, rw_a0, rw_a_up, rw_g_up, rw_k_k, rw_k_a, rw_r_k, rw_ln_w, rw_ln_b, rw_proj,
               w_out, norm2_g, ffn_w_gate, ffn_w_up, ffn_w_down)
    layers = [_layer_weights(l, *weights) for l in range(norm1_g.shape[0])]
    return (_trunk(x_prompt, layers), _trunk(x_sample, layers))
```

```python
import functools

import jax
import jax.numpy as jnp
import numpy as np
from jax import lax
from jax.experimental import pallas as pl
from jax.experimental.pallas import tpu as pltpu

F32 = jnp.float32
BF16 = jnp.bfloat16

D_MODEL = 1024
GRID_W = 64
N_BRANCH = 3
NORM_EPS = 1e-6

NA_HEADS = 8
NA_HEAD_DIM = 64
NA_WIDTH = NA_HEADS * NA_HEAD_DIM
NA_WIN_ROWS = 8
NA_WIN_COLS = 16
NA_RPB_ROWS = 2 * NA_WIN_ROWS - 1
NA_RPB_COLS = 2 * NA_WIN_COLS - 1
NA_MASK = -1e30

MLA_HEADS = 8
MLA_NOPE = 64
MLA_ROPE = 32
MLA_QK = MLA_NOPE + MLA_ROPE
MLA_V = 64
MLA_WIDTH = MLA_HEADS * MLA_V
MLA_Q_RANK = 256
MLA_KV_RANK = 128
MLA_HEAD_PAD = 128
ROPE_THETA = 10000.0

RW_HEADS = 8
RW_HEAD_DIM = 64
RW_WIDTH = RW_HEADS * RW_HEAD_DIM
RW_DECAY_RANK = 64
RW_A_RANK = 64
RW_G_RANK = 128
RW_LN_EPS = 64e-5
RW_IN = 3 * RW_WIDTH + 2 * RW_DECAY_RANK + 2 * RW_A_RANK + RW_G_RANK

D_FF = 2816

VMEM_LIMIT_BYTES = 56 * 1024 * 1024
LANES = 128


def _params(*sem):
    return pltpu.CompilerParams(dimension_semantics=sem, vmem_limit_bytes=VMEM_LIMIT_BYTES)


def _tile(n, pref, mult=8):
    if n <= pref:
        return n
    t = (pref // mult) * mult
    while t >= mult:
        if n % t == 0:
            return t
        t -= mult
    return n


def _resident(a):
    return pl.BlockSpec(a.shape, lambda *_: (0,) * a.ndim, pipeline_mode=pl.Buffered(1))


def _split2(x):
    hi = x.astype(BF16)
    return hi, (x - hi.astype(F32)).astype(BF16)


def _seg_sum(x, ones_bd):
    hi, lo = _split2(x)
    return jnp.dot(hi, ones_bd, preferred_element_type=F32) + jnp.dot(lo, ones_bd, preferred_element_type=F32)


def _head_ones(width, head_dim):
    head = np.arange(width) // head_dim
    return jnp.asarray(head[:, None] == head[None, :], BF16)


def _na_prep_kernel(x_ref, gq_ref, gk_ref, ones_ref, q_ref, k_ref, v_ref):
    inv_n = 1.0 / NA_HEAD_DIM

    def normed(t, g):
        return (t * lax.rsqrt(_seg_sum(t * t, ones_ref[...]) * inv_n + NORM_EPS) * g).astype(BF16)

    q = normed(x_ref[0, :, 0:NA_WIDTH].astype(F32), gq_ref[...])
    k = normed(x_ref[0, :, NA_WIDTH:2 * NA_WIDTH].astype(F32), gk_ref[...])
    for h in range(NA_HEADS):
        sl = slice(h * NA_HEAD_DIM, (h + 1) * NA_HEAD_DIM)
        q_ref[0, h] = q[:, sl]
        k_ref[0, h] = k[:, sl]
        v_ref[0, h] = x_ref[0, :, 2 * NA_WIDTH + sl.start:2 * NA_WIDTH + sl.stop]


def na_prep(na_in, gq, gk, b, L):
    tm = _tile(L, 512)
    x3 = na_in.reshape(b, L, 3 * NA_WIDTH)
    hm = jax.ShapeDtypeStruct((b, NA_HEADS, L, NA_HEAD_DIM), BF16)
    hm_spec = pl.BlockSpec((1, NA_HEADS, tm, NA_HEAD_DIM), lambda i, j: (i, 0, j, 0))
    consts = (jnp.tile(gq, NA_HEADS).reshape(1, -1), jnp.tile(gk, NA_HEADS).reshape(1, -1),
              _head_ones(NA_WIDTH, NA_HEAD_DIM))
    return pl.pallas_call(
        _na_prep_kernel,
        grid=(b, L // tm),
        in_specs=[pl.BlockSpec((1, tm, 3 * NA_WIDTH), lambda i, j: (i, j, 0))] + [_resident(a) for a in consts],
        out_specs=[hm_spec, hm_spec, hm_spec],
        out_shape=[hm, hm, hm],
        compiler_params=_params("parallel", "parallel"),
        name="na_prep",
    )(x3, *consts)


NA_ROWS_PER_STEP = 8
NA_WIN_BLOCKS = 3
NA_ROWS_PER_ITER = 4
NA_BLOCK = NA_ROWS_PER_STEP * GRID_W
NA_KEYS = NA_WIN_ROWS * GRID_W


def _na_window_block(i, nblk):
    return jnp.clip(i - 1, 0, nblk - NA_WIN_BLOCKS)


def _na_attn_kernel(q_ref, k_ref, v_ref, bias_ref, o_ref, *, rows):
    i = pl.program_id(1)
    base_row = _na_window_block(i, rows // NA_ROWS_PER_STEP) * NA_ROWS_PER_STEP

    def rows_body(j, carry):
        work = []
        for jj in range(NA_ROWS_PER_ITER):
            jr = j * NA_ROWS_PER_ITER + jj
            r = i * NA_ROWS_PER_STEP + jr
            rs = jnp.clip(r - NA_WIN_ROWS // 2, 0, rows - NA_WIN_ROWS)
            off = pl.multiple_of((rs - base_row) * GRID_W, GRID_W)
            qoff = pl.multiple_of(jr * GRID_W, GRID_W)
            work += [(h, off, r - rs, qoff) for h in range(NA_HEADS)]
        s = [lax.dot_general(q_ref[0, h, pl.ds(qoff, GRID_W), :], k_ref[0, h, pl.ds(off, NA_KEYS), :],
                             (((1,), (1,)), ((), ())), preferred_element_type=F32) + bias_ref[pat, h]
             for h, off, pat, qoff in work]
        p = [jnp.exp(x - jnp.max(x, axis=-1, keepdims=True)) for x in s]
        l = [jnp.sum(x, axis=-1, keepdims=True) for x in p]
        o = [jnp.dot(x.astype(BF16), v_ref[0, h, pl.ds(off, NA_KEYS), :], preferred_element_type=F32) / y
             for x, y, (h, off, _, _) in zip(p, l, work)]
        for x, (h, _, _, qoff) in zip(o, work):
            o_ref[0, pl.ds(qoff, GRID_W), h * NA_HEAD_DIM:(h + 1) * NA_HEAD_DIM] = x.astype(o_ref.dtype)
        return carry

    lax.fori_loop(0, NA_ROWS_PER_STEP // NA_ROWS_PER_ITER, rows_body, 0)


def _na_bias_table(rpb):
    pat = np.arange(NA_WIN_ROWS)[:, None]
    w = np.arange(NA_WIN_ROWS)[None, :]
    dr_idx = w - pat + NA_WIN_ROWS - 1
    qc = np.arange(GRID_W)[:, None]
    kc = np.arange(GRID_W)[None, :]
    win_start = np.clip(qc - NA_WIN_COLS // 2, 0, GRID_W - NA_WIN_COLS)
    ok = (kc >= win_start) & (kc < win_start + NA_WIN_COLS)
    dc_idx = np.clip(kc - qc + NA_WIN_COLS - 1, 0, NA_RPB_COLS - 1)
    t = rpb.astype(F32)[:, dr_idx][:, :, :, dc_idx]
    t = jnp.where(jnp.asarray(ok)[None, None, None], t, NA_MASK)
    t = jnp.transpose(t, (1, 0, 3, 2, 4))
    return t.reshape(NA_WIN_ROWS, NA_HEADS, GRID_W, NA_KEYS)


def na_attention(q, k, v, bias):
    b, _, L, _ = q.shape
    rows = L // GRID_W
    assert rows % NA_ROWS_PER_STEP == 0
    nblk = rows // NA_ROWS_PER_STEP
    assert nblk >= NA_WIN_BLOCKS
    cur = pl.BlockSpec((1, NA_HEADS, NA_BLOCK, NA_HEAD_DIM), lambda bi, i: (bi, 0, i, 0))
    window = pl.BlockSpec((pl.Element(1), pl.Element(NA_HEADS), pl.Element(NA_WIN_BLOCKS * NA_BLOCK),
                           pl.Element(NA_HEAD_DIM)),
                          lambda bi, i: (bi, 0, _na_window_block(i, nblk) * NA_BLOCK, 0))
    return pl.pallas_call(
        functools.partial(_na_attn_kernel, rows=rows),
        grid=(b, nblk),
        in_specs=[cur, window, window, _resident(bias)],
        out_specs=pl.BlockSpec((1, NA_BLOCK, NA_WIDTH), lambda bi, i: (bi, i, 0)),
        out_shape=jax.ShapeDtypeStruct((b, L, NA_WIDTH), BF16),
        compiler_params=_params("parallel", "parallel"),
        name="na_attn",
    )(q, k, v, bias)


def _rope_tables(L):
    t = np.arange(L)
    row = (t // GRID_W).astype(np.float32)
    col = (t % GRID_W).astype(np.float32)
    n_freq = MLA_ROPE // 4
    inv_freq = jnp.asarray(ROPE_THETA, F32) ** (-jnp.arange(n_freq, dtype=F32) / n_freq)
    ang = jnp.concatenate([jnp.asarray(row)[:, None] * inv_freq, jnp.asarray(col)[:, None] * inv_freq], axis=-1)
    cos, sin = jnp.cos(ang), jnp.sin(ang)
    ones = jnp.ones((L, MLA_NOPE), F32)
    zpad = jnp.zeros((L, MLA_HEAD_PAD - MLA_QK), F32)
    znope = jnp.zeros((L, MLA_NOPE), F32)
    c_tab = jnp.concatenate([ones, cos, cos, zpad], axis=-1)
    s_tab = jnp.concatenate([znope, sin, sin, zpad], axis=-1)
    return c_tab, s_tab


def _mla_prep_kernel(x_ref, gcq_ref, gckv_ref, wuq_ref, wuqr_ref, wuk_ref, wuv_ref, vones_ref, gq_ref, gk_ref,
                     c_ref, s_ref, q_ref, k_ref, v_ref):
    lo = MLA_Q_RANK + MLA_KV_RANK
    cq = x_ref[0, :, 0:MLA_Q_RANK].astype(F32)
    ckv = x_ref[0, :, MLA_Q_RANK:lo].astype(F32)
    kr = x_ref[0, :, lo:lo + MLA_HEAD_PAD].astype(F32)
    kr_rot = x_ref[0, :, lo + MLA_HEAD_PAD:].astype(F32)
    cqn = (cq * lax.rsqrt(jnp.mean(cq * cq, axis=-1, keepdims=True) + NORM_EPS) * gcq_ref[...]).astype(BF16)
    ckvn = (ckv * lax.rsqrt(jnp.mean(ckv * ckv, axis=-1, keepdims=True) + NORM_EPS) * gckv_ref[...]).astype(BF16)
    q_raw = jnp.dot(cqn, wuq_ref[...], preferred_element_type=F32)
    q_rot = jnp.dot(cqn, wuqr_ref[...], preferred_element_type=F32)
    k_raw = jnp.dot(ckvn, wuk_ref[...], preferred_element_type=F32)
    v_ref[0] = (jnp.dot(ckvn, wuv_ref[...], preferred_element_type=F32) + vones_ref[...]).astype(BF16)
    s = s_ref[...]
    gqc = gq_ref[...] * c_ref[...]
    gkc = gk_ref[...] * c_ref[...]
    kr_rot_s = kr_rot * s

    def norm_rope(t, gc, rot_s):
        inv = lax.rsqrt(jnp.sum(t * t, axis=-1, keepdims=True) * (1.0 / MLA_QK) + NORM_EPS)
        return (inv * (t * gc + rot_s)).astype(BF16)

    for h in range(MLA_HEADS):
        sl = slice(h * MLA_HEAD_PAD, (h + 1) * MLA_HEAD_PAD)
        q_ref[0, :, sl] = norm_rope(q_raw[:, sl], gqc, q_rot[:, sl] * s)
        k_ref[0, :, sl] = norm_rope(k_raw[:, sl] + kr, gkc, kr_rot_s)


def mla_prep(mla_in, lw, tabs, b, L):
    tm = _tile(L, 512)
    width = MLA_HEADS * MLA_HEAD_PAD
    in_w = mla_in.shape[-1]
    x3 = mla_in.reshape(b, L, in_w)
    tab_spec = pl.BlockSpec((tm, MLA_HEAD_PAD), lambda i, j: (j, 0))
    weights = (lw["gcq"], lw["gckv"], lw["wuq"], lw["wuq_rot"], lw["wuk"], lw["wuv"], lw["v_ones"], lw["gq"], lw["gk"])
    slab_spec = pl.BlockSpec((1, tm, width), lambda i, j: (i, j, 0))
    slab = jax.ShapeDtypeStruct((b, L, width), BF16)
    return pl.pallas_call(
        _mla_prep_kernel,
        grid=(b, L // tm),
        in_specs=[pl.BlockSpec((1, tm, in_w), lambda i, j: (i, j, 0))]
        + [_resident(a) for a in weights] + [tab_spec] * 2,
        out_specs=[slab_spec, slab_spec, slab_spec],
        out_shape=[slab, slab, slab],
        compiler_params=_params("parallel", "parallel"),
        name="mla_prep",
    )(x3, *weights, *tabs)


def _mla_flash_kernel(q_ref, k_ref, v_ref, o_ref, m_ref, acc_ref):
    ki = pl.program_id(2)
    tk = k_ref.shape[1]

    @pl.when(ki == 0)
    def _():
        m_ref[...] = jnp.full(m_ref.shape, -jnp.inf, F32)
        acc_ref[...] = jnp.zeros(acc_ref.shape, F32)

    def scores(h):
        sl = slice(h * MLA_HEAD_PAD, (h + 1) * MLA_HEAD_PAD)
        return lax.dot_general(q_ref[0, :, sl], k_ref[0, :, sl], (((1,), (1,)), ((), ())),
                               preferred_element_type=F32)

    s_next = scores(0)
    for h in range(MLA_HEADS):
        sl = slice(h * MLA_HEAD_PAD, (h + 1) * MLA_HEAD_PAD)
        s = s_next
        if h + 1 < MLA_HEADS:
            s_next = scores(h + 1)
        m_prev = m_ref[h]
        m_next = jnp.maximum(m_prev, jnp.max(s, axis=-1, keepdims=True))
        alpha = jnp.exp2(m_prev - m_next)
        p = jnp.exp2(s - jnp.concatenate([m_next] * (tk // LANES), axis=1))
        acc_ref[h] = alpha * acc_ref[h] + jnp.dot(p.astype(BF16), v_ref[0, :, sl], preferred_element_type=F32)
        m_ref[h] = m_next

    @pl.when(ki == pl.num_programs(2) - 1)
    def _():
        lane = lax.broadcasted_iota(jnp.int32, (acc_ref.shape[1], LANES), 1)

        def normalised(h):
            acc = acc_ref[h]
            return acc / pltpu.roll(acc, MLA_V, 1)

        for hp in range(MLA_HEADS // 2):
            pair = jnp.where(lane < MLA_V, normalised(2 * hp), pltpu.roll(normalised(2 * hp + 1), MLA_V, 1))
            o_ref[0, :, hp * LANES:(hp + 1) * LANES] = pair.astype(o_ref.dtype)


def mla_flash(q, k, v, tq_pref=1024, tk_pref=2048):
    b, L, width = q.shape
    tq = _tile(L, tq_pref)
    tk = _tile(L, tk_pref, LANES)
    return pl.pallas_call(
        _mla_flash_kernel,
        grid=(b, L // tq, L // tk),
        in_specs=[pl.BlockSpec((1, tq, width), lambda bi, qi, ki: (bi, qi, 0)),
                  pl.BlockSpec((1, tk, width), lambda bi, qi, ki: (bi, ki, 0)),
                  pl.BlockSpec((1, tk, width), lambda bi, qi, ki: (bi, ki, 0))],
        out_specs=pl.BlockSpec((1, tq, MLA_WIDTH), lambda bi, qi, ki: (bi, qi, 0)),
        out_shape=jax.ShapeDtypeStruct((b, L, MLA_WIDTH), BF16),
        scratch_shapes=[pltpu.VMEM((MLA_HEADS, tq, LANES), F32),
                        pltpu.VMEM((MLA_HEADS, tq, LANES), F32)],
        compiler_params=_params("parallel", "parallel", "arbitrary"),
        name="mla_flash",
    )(q, k, v)


def _rope_partner(w, g):
    half = MLA_ROPE // 2
    wg = w.astype(F32) * g
    return jnp.concatenate([-wg[..., half:], wg[..., :half]], axis=-1)


def _mla_layer_weights(cq_norm, ckv_norm, w_uq, w_ukv, q_norm, k_norm):
    pad = MLA_HEAD_PAD - MLA_QK
    gq = q_norm.astype(F32) * (MLA_QK ** -0.5 * np.log2(np.e))
    wuq3 = w_uq.reshape(MLA_Q_RANK, MLA_HEADS, MLA_QK)
    wuq = jnp.pad(wuq3, ((0, 0), (0, 0), (0, pad)))
    wuq_rot = jnp.pad(_rope_partner(wuq3[:, :, MLA_NOPE:], gq[MLA_NOPE:]), ((0, 0), (0, 0), (MLA_NOPE, pad)))
    wukv = w_ukv.reshape(MLA_KV_RANK, MLA_HEADS, MLA_NOPE + MLA_V)
    wuk = jnp.pad(wukv[:, :, :MLA_NOPE], ((0, 0), (0, 0), (0, MLA_HEAD_PAD - MLA_NOPE)))
    wuv = jnp.pad(wukv[:, :, MLA_NOPE:], ((0, 0), (0, 0), (0, MLA_HEAD_PAD - MLA_V)))
    v_ones = np.tile(np.arange(MLA_HEAD_PAD) >= MLA_V, MLA_HEADS).astype(np.float32)
    return {
        "v_ones": jnp.asarray(v_ones).reshape(1, -1),
        "gcq": cq_norm.reshape(1, -1).astype(F32),
        "gckv": ckv_norm.reshape(1, -1).astype(F32),
        "wuq": wuq.reshape(MLA_Q_RANK, -1).astype(BF16),
        "wuq_rot": wuq_rot.reshape(MLA_Q_RANK, -1).astype(BF16),
        "wuk": wuk.reshape(MLA_KV_RANK, -1).astype(BF16),
        "wuv": wuv.reshape(MLA_KV_RANK, -1).astype(BF16),
        "gq": jnp.pad(gq, (0, pad)).reshape(1, -1),
        "gk": jnp.pad(k_norm.astype(F32), (0, pad)).reshape(1, -1),
    }


RW_CHUNK = 64
RW_SUB = 16
RW_QUAD = 4
RW_QW = RW_QUAD * RW_HEAD_DIM
RW_SPLITS = (RW_WIDTH, 2 * RW_WIDTH, 3 * RW_WIDTH, 3 * RW_WIDTH + 2 * RW_DECAY_RANK,
             3 * RW_WIDTH + 2 * RW_DECAY_RANK + 2 * RW_A_RANK)
HALO = 8


def _dot3(x, w_hi, w_lo):
    hi, lo = _split2(x)
    return (jnp.dot(hi, w_hi, preferred_element_type=F32) + jnp.dot(lo, w_hi, preferred_element_type=F32)
            + jnp.dot(hi, w_lo, preferred_element_type=F32))


def _chunk_cumsum(x, tri):
    hi = x.astype(BF16)
    r1 = x - hi.astype(F32)
    mid = r1.astype(BF16)
    lo = (r1 - mid.astype(F32)).astype(BF16)
    return (jnp.dot(tri, hi, preferred_element_type=F32) + jnp.dot(tri, mid, preferred_element_type=F32)
            + jnp.dot(tri, lo, preferred_element_type=F32))


def _rw_prep(p, prev_row, next_row, mu_ref, wup_hi, wup_lo, w0_ref, aup_hi, aup_lo, a0_ref,
             gup_hi, gup_lo, kk_ref, ka_ref, ones_ref, trif_ref, trib_ref,
             r_ref, v_ref, kkn_ref, g_ref, lw_ref, kd_ref, a_ref, cum_ref):
    tm = p.shape[0]
    row = lax.broadcasted_iota(jnp.int32, p.shape, 0)
    prev = jnp.where(row == 0, prev_row, pltpu.roll(p, 1, 0))
    nxt = jnp.where(row == tm - 1, next_row, pltpu.roll(p, tm - 1, 0))
    pm = p + mu_ref[...] * (0.5 * (prev + nxt) - p)
    s0, s1, s2, s3, s4 = RW_SPLITS
    r, k, v = pm[:, 0:s0], pm[:, s0:s1], pm[:, s1:s2]
    wd = jnp.tanh(pm[:, s2:s3])
    ad = pm[:, s3:s4]
    gd = pm[:, s4:]
    w_raw = w0_ref[...] + _dot3(wd, wup_hi[...], wup_lo[...])
    lw = -np.float32(np.exp(-0.5)) * jax.nn.sigmoid(w_raw)
    a = jax.nn.sigmoid(a0_ref[...] + _dot3(ad, aup_hi[...], aup_lo[...]))
    g_ref[...] = _dot3(jax.nn.sigmoid(gd), gup_hi[...], gup_lo[...])
    kk = k * kk_ref[...]
    kkn_ref[...] = kk * lax.rsqrt(_seg_sum(kk * kk, ones_ref[...]) + 1e-12)
    for d, tri_ref in enumerate((trif_ref, trib_ref)):
        sl = slice(d * RW_WIDTH, (d + 1) * RW_WIDTH)
        lw_ref[d] = lw[:, sl]
        cum_ref[d] = _chunk_cumsum(lw[:, sl], tri_ref[...])
        a_ref[d] = a[:, sl]
        kd_ref[d] = k * (1.0 + (a[:, sl] - 1.0) * ka_ref[...])
    r_ref[...] = r
    v_ref[...] = v


N_RW_PREP_WEIGHTS = 14
N_RW_PREP_OUTPUTS = 8


def _in_proj_rw_kernel(x_ref, xp_ref, xn_ref, g_ref, wna_ref, wmla_ref, wgate_ref, wrw_ref, *refs, tiles_per_seq):
    rw_weights = refs[:N_RW_PREP_WEIGHTS]
    ona_ref, omla_ref, ogate_ref = refs[N_RW_PREP_WEIGHTS:N_RW_PREP_WEIGHTS + 3]
    rw_outs = refs[N_RW_PREP_WEIGHTS + 3:]
    pos = pl.program_id(0) % tiles_per_seq

    def normed(x):
        return (x * lax.rsqrt(jnp.mean(x * x, axis=-1, keepdims=True) + NORM_EPS) * g_ref[...]).astype(BF16)

    h = normed(x_ref[...])
    ona_ref[...] = jnp.dot(h, wna_ref[...], preferred_element_type=F32).astype(ona_ref.dtype)
    omla_ref[...] = jnp.dot(h, wmla_ref[...], preferred_element_type=F32).astype(omla_ref.dtype)
    ogate_ref[...] = jnp.dot(h, wgate_ref[...], preferred_element_type=F32).astype(ogate_ref.dtype)
    p = jnp.dot(h, wrw_ref[...], preferred_element_type=F32)
    halo = normed(jnp.concatenate([xp_ref[...], xn_ref[...]], axis=0))
    p_halo = jnp.dot(halo, wrw_ref[...], preferred_element_type=F32)
    prev_row = jnp.where(pos > 0, p_halo[HALO - 1:HALO], 0.0)
    next_row = jnp.where(pos < tiles_per_seq - 1, p_halo[HALO:HALO + 1], 0.0)
    _rw_prep(p, prev_row, next_row, *rw_weights, *rw_outs)


def in_proj_rw(x, g, w_na, w_mla, w_gate, w_rw, lw, L):
    t, d = x.shape
    tm = _tile(L, 256, RW_CHUNK)
    nh = t // HALO
    g = g.reshape(1, d)
    t_i = np.arange(tm)[:, None]
    s_i = np.arange(tm)[None, :]
    same_chunk = (t_i // RW_CHUNK) == (s_i // RW_CHUNK)
    tri_f = jnp.asarray(same_chunk & (s_i <= t_i), BF16)
    tri_b = jnp.asarray(same_chunk & (s_i >= t_i), BF16)
    rw_weights = (lw["mu"], lw["wup_hi"], lw["wup_lo"], lw["w0"], lw["aup_hi"], lw["aup_lo"], lw["a0"],
                  lw["gup_hi"], lw["gup_lo"], lw["k_k"], lw["k_a"], lw["ones_bd"], tri_f, tri_b)
    assert len(rw_weights) == N_RW_PREP_WEIGHTS
    proj = (w_na, w_mla, w_gate)
    one = jax.ShapeDtypeStruct((t, RW_WIDTH), F32)
    two = jax.ShapeDtypeStruct((2, t, RW_WIDTH), F32)
    one_spec = pl.BlockSpec((tm, RW_WIDTH), lambda i: (i, 0))
    two_spec = pl.BlockSpec((2, tm, RW_WIDTH), lambda i: (0, i, 0))
    return pl.pallas_call(
        functools.partial(_in_proj_rw_kernel, tiles_per_seq=L // tm),
        grid=(t // tm,),
        in_specs=[pl.BlockSpec((tm, d), lambda i: (i, 0)),
                  pl.BlockSpec((HALO, d), lambda i: (jnp.maximum(i * (tm // HALO) - 1, 0), 0)),
                  pl.BlockSpec((HALO, d), lambda i: (jnp.minimum((i + 1) * (tm // HALO), nh - 1), 0)),
                  _resident(g)] + [_resident(w) for w in proj + (w_rw,) + rw_weights],
        out_specs=[pl.BlockSpec((tm, w.shape[1]), lambda i: (i, 0)) for w in proj]
        + [one_spec] * 4 + [two_spec] * 4,
        out_shape=[jax.ShapeDtypeStruct((t, w.shape[1]), BF16) for w in proj] + [one] * 4 + [two] * 4,
        compiler_params=_params("parallel"),
        name="in_proj_rw",
    )(x, x, x, g, *proj, w_rw, *rw_weights)


def _rw_masks(reverse):
    C, NQ = RW_CHUNK, RW_QUAD
    wt = lax.broadcasted_iota(jnp.int32, (C, NQ * C), 0)
    ws = lax.broadcasted_iota(jnp.int32, (C, NQ * C), 1) & (C - 1)
    strict = (ws > wt) if reverse else (ws < wt)
    incl = (ws >= wt) if reverse else (ws <= wt)
    same = (ws // RW_SUB) == (wt // RW_SUB)
    eye = jnp.where(ws == wt, 1.0, 0.0)
    return strict, incl, same, eye


def _rw_chunks(chains, bdm, bdm_b):
    C, NQ = RW_CHUNK, RW_QUAD
    n = len(chains)
    masks = {rev: _rw_masks(rev) for rev in sorted({c["reverse"] for c in chains})}
    strict = [masks[c["reverse"]][0] for c in chains]
    incl = [masks[c["reverse"]][1] for c in chains]
    same = [masks[c["reverse"]][2] for c in chains]
    eye = [masks[c["reverse"]][3] for c in chains]
    ids = range(n)

    def dot(a, b):
        return jnp.dot(a, b, preferred_element_type=F32)

    def dot_nt(a, b):
        return lax.dot_general(a, b, (((1,), (1,)), ((), ())), preferred_element_type=F32)

    def bd(x):
        return jnp.concatenate([x] * NQ, axis=0) * bdm_b

    def mm(ms, xs):
        return [dot(m.astype(BF16), bd(x.astype(BF16))) for m, x in zip(ms, xs)]

    lw = [c["lw"] for c in chains]
    cum = [c["cum"] for c in chains]
    tot =[cum[i][0:1] if chains[i]["reverse"] else cum[i][C - 1:C] for i in ids]
    pinv = [jnp.exp(-x) for x in cum]
    pend = [jnp.exp(t - x) for t, x in zip(tot, cum)]
    kka = [c["kk"] * c["a"] for c in chains]
    kap = [(chains[i]["kk"] * jnp.exp(cum[i] - lw[i])).astype(BF16) for i in ids]
    bet = [(kka[i] * pinv[i]).astype(BF16) for i in ids]
    kt = [(chains[i]["kd"] * pinv[i]).astype(BF16) for i in ids]
    rt = [(chains[i]["r"] * jnp.exp(cum[i])).astype(BF16) for i in ids]
    v = [c["v"] for c in chains]

    x2 = [jnp.concatenate([kap[i], rt[i]], axis=0) for i in ids]
    y2 = [jnp.concatenate([bd(bet[i]), bd(kt[i])], axis=0) for i in ids]
    aw = [dot_nt(x2[i], y2[i]) for i in ids]
    zz = [dot_nt(x2[i], chains[i]["zt"].astype(BF16)) for i in ids]
    a_ab = [jnp.where(strict[i], aw[i][:C, :NQ * C], 0.0) for i in ids]
    a_ak = [jnp.where(strict[i], aw[i][:C, NQ * C:], 0.0) for i in ids]
    a_rb = [jnp.where(incl[i], aw[i][C:, :NQ * C], 0.0) for i in ids]
    a_rk = [jnp.where(incl[i], aw[i][C:, NQ * C:], 0.0) for i in ids]
    def mm2(tops, bottoms, xs):
        both = mm([jnp.concatenate([a, b], axis=0) for a, b in zip(tops, bottoms)], xs)
        return [x[:C] for x in both], [x[C:] for x in both]

    akv, yv = mm2(a_ak, a_rk, v)
    rhs = [-(zz[i][:C] + akv[i]) for i in ids]
    d = [jnp.where(same[i], a_ab[i], 0.0) for i in ids]
    e = [a_ab[i] - d[i] for i in ids]
    d2 = mm(d, d)
    t = [eye[i] - d[i] for i in ids]
    d4, td = mm2(d2, t, d2)
    t = [x + y for x, y in zip(t, td)]
    d8, td = mm2(d4, t, d4)
    t = [x + y for x, y in zip(t, td)]
    t = [x + y for x, y in zip(t, mm(t, d8))]
    nn = mm(t, e)
    n2 = mm(nn, nn)
    u = mm(t, rhs)
    u = [x + y for x, y in zip(u, mm(n2, u))]
    u = [x - y for x, y in zip(u, mm(nn, u))]
    yu = mm(a_rb, u)
    y = [zz[i][C:] + yv[i] + yu[i] for i in ids]
    lhs = [jnp.concatenate([v[i], u[i]], axis=0).astype(BF16) for i in ids]
    rhs2 = [jnp.concatenate([chains[i]["kd"] * pend[i], kka[i] * pend[i]], axis=0).astype(BF16) for i in ids]
    upd = [lax.dot_general(lhs[i], rhs2[i], (((0,), (0,)), ((), ())), preferred_element_type=F32) for i in ids]
    zt = [chains[i]["zt"] * jnp.exp(tot[i]) + upd[i] * bdm for i in ids]
    return list(zip(y, zt))


RW_SEQS = 4


def _rw_scan_kernel(rf, vf, kkf, lwf, kdf, af, cf, rb, vb, kkb, lwb, kdb, ab, cb, bdm_ref, yf_ref, yb_ref, zt_ref):
    @pl.when(pl.program_id(1) == 0)
    def _():
        zt_ref[...] = jnp.zeros(zt_ref.shape, F32)

    bdm = bdm_ref[...]
    nq = RW_HEADS // RW_QUAD
    names = ("r", "lw", "kd", "v", "kk", "a", "cum")
    chains, outs = [], []
    for bi in range(RW_SEQS):
        for di, (reverse, refs, y_ref) in enumerate(((False, (rf, lwf, kdf, vf, kkf, af, cf), yf_ref),
                                                     (True, (rb, lwb, kdb, vb, kkb, ab, cb), yb_ref))):
            for q in range(nq):
                sl = slice(q * RW_QW, (q + 1) * RW_QW)
                idx = (bi * 2 + di) * nq + q
                chain = {name: ref[bi, :, sl] for name, ref in zip(names, refs)}
                chain["zt"] = zt_ref[idx]
                chain["reverse"] = reverse
                chains.append(chain)
                outs.append((y_ref, bi, sl, idx))
    for (y, zt), (y_ref, bi, sl, idx) in zip(_rw_chunks(chains, bdm, bdm.astype(BF16)), outs):
        y_ref[bi, :, sl] = y
        zt_ref[idx] = zt


def rw_scan(r, v, kk, lw, kd, a, cum, bdm):
    b, L, _ = r.shape
    assert b % RW_SEQS == 0
    nc = L // RW_CHUNK
    blk = (RW_SEQS, RW_CHUNK, RW_WIDTH)
    f1 = pl.BlockSpec(blk, lambda bi, c: (bi, c, 0))
    b1 = pl.BlockSpec(blk, lambda bi, c: (bi, nc - 1 - c, 0))
    f2 = pl.BlockSpec((None,) + blk, lambda bi, c: (0, bi, c, 0))
    b2 = pl.BlockSpec((None,) + blk, lambda bi, c: (1, bi, nc - 1 - c, 0))
    out = jax.ShapeDtypeStruct((b, L, RW_WIDTH), F32)
    return pl.pallas_call(
        _rw_scan_kernel,
        grid=(b // RW_SEQS, nc),
        in_specs=[f1, f1, f1, f2, f2, f2, f2, b1, b1, b1, b2, b2, b2, b2, _resident(bdm)],
        out_specs=[f1, b1],
        out_shape=[out, out],
        scratch_shapes=[pltpu.VMEM((RW_SEQS * 2 * RW_HEADS // RW_QUAD, RW_QW, RW_QW), F32)],
        compiler_params=_params("parallel", "arbitrary"),
        name="rw_scan",
    )(r, v, kk, lw, kd, a, cum, r, v, kk, lw, kd, a, cum, bdm)


def _rw_post_kernel(yf_ref, yb_ref, r_ref, v_ref, kdf_ref, kdb_ref, g_ref, lnw_ref, lnb_ref, rk_ref, ones_ref, o_ref):
    ones_bd = ones_ref[...]
    inv_n = 1.0 / RW_HEAD_DIM
    y = yf_ref[...] + yb_ref[...]
    yc = y - _seg_sum(y, ones_bd) * inv_n
    var = _seg_sum(yc * yc, ones_bd) * inv_n
    y = yc * lax.rsqrt(var + RW_LN_EPS) * lnw_ref[...] + lnb_ref[...]
    kd = kdf_ref[...] + kdb_ref[...]
    bonus = _seg_sum(r_ref[...] * kd * rk_ref[...], ones_bd)
    o_ref[...] = ((y + bonus * v_ref[...]) * g_ref[...]).astype(o_ref.dtype)


def rw_post(yf, yb, r, v, kd, g, lw):
    t = yf.shape[0]
    tm = _tile(t, 512)
    one = pl.BlockSpec((tm, RW_WIDTH), lambda i: (i, 0))
    kd_f = pl.BlockSpec((None, tm, RW_WIDTH), lambda i: (0, i, 0))
    kd_b = pl.BlockSpec((None, tm, RW_WIDTH), lambda i: (1, i, 0))
    weights = (lw["ln_w"], lw["ln_b"], lw["r_k"], lw["ones_bd"])
    return pl.pallas_call(
        _rw_post_kernel,
        grid=(t // tm,),
        in_specs=[one, one, one, one, kd_f, kd_b, one] + [_resident(a) for a in weights],
        out_specs=one,
        out_shape=jax.ShapeDtypeStruct((t, RW_WIDTH), BF16),
        compiler_params=_params("parallel"),
        name="rw_post",
    )(yf, yb, r, v, kd, kd, g, *weights)


def _block_diag2(w):
    z = jnp.zeros_like(w[0])
    return jnp.concatenate([jnp.concatenate([w[0], z], axis=1), jnp.concatenate([z, w[1]], axis=1)], axis=0)


def _hi_lo(w):
    w = w.astype(F32)
    hi = w.astype(BF16)
    return hi, (w - hi.astype(F32)).astype(BF16)


def _rw_layer_weights(mu, w0, w_up, a0, a_up, g_up, k_k, k_a, r_k, ln_w, ln_b):
    wup_hi, wup_lo = _hi_lo(_block_diag2(w_up))
    aup_hi, aup_lo = _hi_lo(_block_diag2(a_up))
    gup_hi, gup_lo = _hi_lo(g_up)

    def row(t):
        return t.reshape(1, -1).astype(F32)

    return {
        "mu": row(mu), "wup_hi": wup_hi, "wup_lo": wup_lo, "w0": row(w0),
        "aup_hi": aup_hi, "aup_lo": aup_lo, "a0": row(a0), "gup_hi": gup_hi, "gup_lo": gup_lo,
        "k_k": row(k_k), "k_a": row(k_a), "r_k": row(r_k), "ln_w": row(ln_w), "ln_b": row(ln_b),
        "ones_bd": _head_ones(RW_WIDTH, RW_HEAD_DIM),
    }


def rwkv7_mix(prepared, lw, bdm, b, L):
    r, v, kk, g, lwd, kd, a, cum = prepared
    t = b * L

    def seq(x):
        return x.reshape(x.shape[:-2] + (b, L, RW_WIDTH))

    yf, yb = rw_scan(seq(r), seq(v), seq(kk), seq(lwd), seq(kd), seq(a), seq(cum), bdm)
    return rw_post(yf.reshape(t, RW_WIDTH), yb.reshape(t, RW_WIDTH), r, v, kd, g, lw)


def _merge_kernel(x_ref, ya_ref, yb_ref, yc_ref, gin_ref, bg_ref, pa_ref, pb_ref, pc_ref, wo_ref, o_ref):
    mixed = None
    for i, (y_ref, p_ref) in enumerate(((ya_ref, pa_ref), (yb_ref, pb_ref), (yc_ref, pc_ref))):
        sl = slice(i * D_MODEL, (i + 1) * D_MODEL)
        gate = jax.nn.sigmoid(gin_ref[:, sl].astype(F32) + bg_ref[:, sl])
        term = gate * jnp.dot(y_ref[...], p_ref[...], preferred_element_type=F32)
        mixed = term if mixed is None else mixed + term
    o_ref[...] = x_ref[...] + jnp.dot(mixed.astype(BF16), wo_ref[...], preferred_element_type=F32)


def merge(x, ya, yb, yc, gate_in, b_gate, pa, pb, pc, wo):
    t = x.shape[0]
    tm = _tile(t, 512)

    def rows(w):
        return pl.BlockSpec((tm, w), lambda i: (i, 0))

    def full(a):
        return pl.BlockSpec(a.shape, lambda i: (0,) * a.ndim)

    return pl.pallas_call(
        _merge_kernel,
        grid=(t // tm,),
        in_specs=[rows(D_MODEL), rows(NA_WIDTH), rows(MLA_WIDTH), rows(RW_WIDTH), rows(N_BRANCH * D_MODEL),
                  full(b_gate), full(pa), full(pb), full(pc), full(wo)],
        out_specs=rows(D_MODEL),
        out_shape=jax.ShapeDtypeStruct((t, D_MODEL), F32),
        compiler_params=_params("parallel"),
        name="merge",
    )(x, ya, yb, yc, gate_in, b_gate, pa, pb, pc, wo)


def _ffn_kernel(x_ref, g_ref, wg_ref, wu_ref, wd_ref, o_ref):
    x = x_ref[...]
    ms = jnp.mean(x * x, axis=-1, keepdims=True)
    h = (x * lax.rsqrt(ms + NORM_EPS) * g_ref[...]).astype(BF16)
    gate = jnp.dot(h, wg_ref[...], preferred_element_type=F32)
    up = jnp.dot(h, wu_ref[...], preferred_element_type=F32)
    act = (gate * jax.nn.sigmoid(gate) * up).astype(BF16)
    o_ref[...] = x + jnp.dot(act, wd_ref[...], preferred_element_type=F32)


def ffn(x, g, wg, wu, wd, tm_pref=512):
    t, d = x.shape
    tm = _tile(t, tm_pref)
    g = g.reshape(1, d)
    return pl.pallas_call(
        _ffn_kernel,
        grid=(t // tm,),
        in_specs=[pl.BlockSpec((tm, d), lambda i: (i, 0))] + [_resident(a) for a in (g, wg, wu, wd)],
        out_specs=pl.BlockSpec((tm, d), lambda i: (i, 0)),
        out_shape=jax.ShapeDtypeStruct((t, d), F32),
        compiler_params=_params("parallel"),
        name="ffn",
    )(x, g, wg, wu, wd)


IN_SIZES = (NA_WIDTH, NA_WIDTH, NA_WIDTH, MLA_Q_RANK, MLA_KV_RANK, MLA_ROPE, RW_IN, N_BRANCH * D_MODEL)
IN_SPLITS = tuple(int(s) for s in np.cumsum(IN_SIZES)[:-1])


def _layer_weights(l, norm1_g, w_in, b_gate, na_q_norm, na_k_norm, na_rpb, na_proj,
                   mla_cq_norm, mla_ckv_norm, mla_w_uq, mla_w_ukv, mla_q_norm, mla_k_norm, mla_proj,
                   rw_mu, rw_w0, rw_w_up, rw_a0, rw_a_up, rw_g_up, rw_k_k, rw_k_a, rw_r_k, rw_ln_w, rw_ln_b, rw_proj,
                   w_out, norm2_g, ffn_w_gate, ffn_w_up, ffn_w_down):
    w = w_in[l]
    slab_pad = ((0, 0), (MLA_NOPE, MLA_HEAD_PAD - MLA_QK))
    w_kr = w[:, IN_SPLITS[4]:IN_SPLITS[5]]
    w_kr_rot = _rope_partner(w_kr, mla_k_norm[l].astype(F32)[MLA_NOPE:])
    w_mla = jnp.concatenate([w[:, IN_SPLITS[2]:IN_SPLITS[4]], jnp.pad(w_kr, slab_pad), jnp.pad(w_kr_rot, slab_pad)],
                            axis=1)
    return {
        "norm1_g": norm1_g[l], "w_na": w[:, :IN_SPLITS[2]].astype(BF16), "w_mla": w_mla.astype(BF16),
        "w_rw": w[:, IN_SPLITS[5]:IN_SPLITS[6]].astype(BF16),
        "w_gate": w[:, IN_SPLITS[6]:].astype(BF16),
        "b_gate": b_gate[l].reshape(1, -1).astype(F32),
        "na_gq": na_q_norm[l].astype(F32) * (NA_HEAD_DIM ** -0.5), "na_gk": na_k_norm[l].astype(F32),
        "na_bias": _na_bias_table(na_rpb[l]),
        "na_proj": na_proj[l].astype(BF16), "mla_proj": mla_proj[l].astype(BF16), "rw_proj": rw_proj[l].astype(BF16),
        "mla": _mla_layer_weights(mla_cq_norm[l], mla_ckv_norm[l], mla_w_uq[l], mla_w_ukv[l], mla_q_norm[l], mla_k_norm[l]),
        "rw": _rw_layer_weights(rw_mu[l], rw_w0[l], rw_w_up[l], rw_a0[l], rw_a_up[l], rw_g_up[l],
                                rw_k_k[l], rw_k_a[l], rw_r_k[l], rw_ln_w[l], rw_ln_b[l]),
        "w_out": w_out[l].astype(BF16), "norm2_g": norm2_g[l],
        "ffn_w_gate": ffn_w_gate[l].astype(BF16), "ffn_w_up": ffn_w_up[l].astype(BF16),
        "ffn_w_down": ffn_w_down[l].astype(BF16),
    }


def _trunk(x, layers):
    b, L, d = x.shape
    t = b * L
    x = x.reshape(t, d)
    tabs = _rope_tables(L)
    bdm = _head_ones(RW_QW, RW_HEAD_DIM).astype(F32)
    for lw in layers:
        na_in, mla_in, gate_in, *rw_prepared = in_proj_rw(x, lw["norm1_g"], lw["w_na"], lw["w_mla"], lw["w_gate"],
                                                          lw["w_rw"], lw["rw"], L)
        qa, ka, va = na_prep(na_in, lw["na_gq"], lw["na_gk"], b, L)
        y_a = na_attention(qa, ka, va, lw["na_bias"]).reshape(t, NA_WIDTH)
        qm, km, vm = mla_prep(mla_in, lw["mla"], tabs, b, L)
        y_b = mla_flash(qm, km, vm).reshape(t, MLA_WIDTH)
        y_c = rwkv7_mix(rw_prepared, lw["rw"], bdm, b, L)
        x = merge(x, y_a, y_b, y_c, gate_in, lw["b_gate"], lw["na_proj"], lw["mla_proj"], lw["rw_proj"], lw["w_out"])
        x = ffn(x, lw["norm2_g"], lw["ffn_w_gate"], lw["ffn_w_up"], lw["ffn_w_down"])
    return x.reshape(b, L, d)


def kernel(x_prompt, x_sample, norm1_g, w_in, b_gate, na_q_norm, na_k_norm, na_rpb, na_proj, mla_cq_norm, mla_ckv_norm, mla_w_uq, mla_w_ukv, mla_q_norm, mla_k_norm, mla_proj, rw_mu, rw_w0, rw_w_up, rw_a0, rw_a_up, rw_g_up, rw_k_k, rw_k_a, rw_r_k, rw_ln_w, rw_ln_b, rw_proj, w_out, norm2_g, ffn_w_gate, ffn_w_up, ffn_w_down):
    weights = (norm1_g, w_in, b_gate, na_q_norm, na_k_norm, na_rpb, na_proj,
               mla_cq_norm, mla_ckv_norm, mla_w_uq, mla_w_ukv, mla_q_norm, mla_k_norm, mla_proj,
               rw_mu, rw_w0, rw_w_up, rw_a0, rw_a_up, rw_g_up, rw_k_k, rw_k_a, rw_r_k, rw_ln_w, rw_ln_b, rw_proj,
               w_out, norm2_g, ffn_w_gate, ffn_w_up, ffn_w_down)
    layers = [_layer_weights(l, *weights) for l in range(norm1_g.shape[0])]
    return (_trunk(x_prompt, layers), _trunk(x_sample, layers))
```

```python
import functools

import jax
import jax.numpy as jnp
import numpy as np
from jax import lax
from jax.experimental import pallas as pl
from jax.experimental.pallas import tpu as pltpu

F32 = jnp.float32
BF16 = jnp.bfloat16

D_MODEL = 1024
GRID_W = 64
N_BRANCH = 3
NORM_EPS = 1e-6

NA_HEADS = 8
NA_HEAD_DIM = 64
NA_WIDTH = NA_HEADS * NA_HEAD_DIM
NA_WIN_ROWS = 8
NA_WIN_COLS = 16
NA_RPB_ROWS = 2 * NA_WIN_ROWS - 1
NA_RPB_COLS = 2 * NA_WIN_COLS - 1
NA_MASK = -1e30

MLA_HEADS = 8
MLA_NOPE = 64
MLA_ROPE = 32
MLA_QK = MLA_NOPE + MLA_ROPE
MLA_V = 64
MLA_WIDTH = MLA_HEADS * MLA_V
MLA_Q_RANK = 256
MLA_KV_RANK = 128
MLA_HEAD_PAD = 128
ROPE_THETA = 10000.0

RW_HEADS = 8
RW_HEAD_DIM = 64
RW_WIDTH = RW_HEADS * RW_HEAD_DIM
RW_DECAY_RANK = 64
RW_A_RANK = 64
RW_G_RANK = 128
RW_LN_EPS = 64e-5
RW_IN = 3 * RW_WIDTH + 2 * RW_DECAY_RANK + 2 * RW_A_RANK + RW_G_RANK

D_FF = 2816

VMEM_LIMIT_BYTES = 56 * 1024 * 1024
LANES = 128


def _params(*sem):
    return pltpu.CompilerParams(dimension_semantics=sem, vmem_limit_bytes=VMEM_LIMIT_BYTES)


def _tile(n, pref, mult=8):
    if n <= pref:
        return n
    t = (pref // mult) * mult
    while t >= mult:
        if n % t == 0:
            return t
        t -= mult
    return n


def _resident(a):
    return pl.BlockSpec(a.shape, lambda *_: (0,) * a.ndim, pipeline_mode=pl.Buffered(1))


def _split2(x):
    hi = x.astype(BF16)
    return hi, (x - hi.astype(F32)).astype(BF16)


def _seg_sum(x, ones_bd):
    hi, lo = _split2(x)
    return jnp.dot(hi, ones_bd, preferred_element_type=F32) + jnp.dot(lo, ones_bd, preferred_element_type=F32)


def _head_ones(width, head_dim):
    head = np.arange(width) // head_dim
    return jnp.asarray(head[:, None] == head[None, :], BF16)


def _na_prep_kernel(x_ref, gq_ref, gk_ref, ones_ref, q_ref, k_ref, v_ref):
    inv_n = 1.0 / NA_HEAD_DIM

    def normed(t, g):
        return (t * lax.rsqrt(_seg_sum(t * t, ones_ref[...]) * inv_n + NORM_EPS) * g).astype(BF16)

    q = normed(x_ref[0, :, 0:NA_WIDTH].astype(F32), gq_ref[...])
    k = normed(x_ref[0, :, NA_WIDTH:2 * NA_WIDTH].astype(F32), gk_ref[...])
    for h in range(NA_HEADS):
        sl = slice(h * NA_HEAD_DIM, (h + 1) * NA_HEAD_DIM)
        q_ref[0, h] = q[:, sl]
        k_ref[0, h] = k[:, sl]
        v_ref[0, h] = x_ref[0, :, 2 * NA_WIDTH + sl.start:2 * NA_WIDTH + sl.stop]


def na_prep(na_in, gq, gk, b, L):
    tm = _tile(L, 512)
    x3 = na_in.reshape(b, L, 3 * NA_WIDTH)
    hm = jax.ShapeDtypeStruct((b, NA_HEADS, L, NA_HEAD_DIM), BF16)
    hm_spec = pl.BlockSpec((1, NA_HEADS, tm, NA_HEAD_DIM), lambda i, j: (i, 0, j, 0))
    consts = (jnp.tile(gq, NA_HEADS).reshape(1, -1), jnp.tile(gk, NA_HEADS).reshape(1, -1),
              _head_ones(NA_WIDTH, NA_HEAD_DIM))
    return pl.pallas_call(
        _na_prep_kernel,
        grid=(b, L // tm),
        in_specs=[pl.BlockSpec((1, tm, 3 * NA_WIDTH), lambda i, j: (i, j, 0))] + [_resident(a) for a in consts],
        out_specs=[hm_spec, hm_spec, hm_spec],
        out_shape=[hm, hm, hm],
        compiler_params=_params("parallel", "parallel"),
        name="na_prep",
    )(x3, *consts)


NA_ROWS_PER_STEP = 8
NA_WIN_BLOCKS = 3
NA_ROWS_PER_ITER = 4
NA_BLOCK = NA_ROWS_PER_STEP * GRID_W
NA_KEYS = NA_WIN_ROWS * GRID_W


def _na_window_block(i, nblk):
    return jnp.clip(i - 1, 0, nblk - NA_WIN_BLOCKS)


def _na_attn_kernel(q_ref, k_ref, v_ref, bias_ref, o_ref, *, rows):
    i = pl.program_id(1)
    base_row = _na_window_block(i, rows // NA_ROWS_PER_STEP) * NA_ROWS_PER_STEP

    def rows_body(j, carry):
        work = []
        for jj in range(NA_ROWS_PER_ITER):
            jr = j * NA_ROWS_PER_ITER + jj
            r = i * NA_ROWS_PER_STEP + jr
            rs = jnp.clip(r - NA_WIN_ROWS // 2, 0, rows - NA_WIN_ROWS)
            off = pl.multiple_of((rs - base_row) * GRID_W, GRID_W)
            qoff = pl.multiple_of(jr * GRID_W, GRID_W)
            work += [(h, off, r - rs, qoff) for h in range(NA_HEADS)]
        s = [lax.dot_general(q_ref[0, h, pl.ds(qoff, GRID_W), :], k_ref[0, h, pl.ds(off, NA_KEYS), :],
                             (((1,), (1,)), ((), ())), preferred_element_type=F32) + bias_ref[pat, h]
             for h, off, pat, qoff in work]
        p = [jnp.exp(x - jnp.max(x, axis=-1, keepdims=True)) for x in s]
        l = [jnp.sum(x, axis=-1, keepdims=True) for x in p]
        o = [jnp.dot(x.astype(BF16), v_ref[0, h, pl.ds(off, NA_KEYS), :], preferred_element_type=F32) / y
             for x, y, (h, off, _, _) in zip(p, l, work)]
        for x, (h, _, _, qoff) in zip(o, work):
            o_ref[0, pl.ds(qoff, GRID_W), h * NA_HEAD_DIM:(h + 1) * NA_HEAD_DIM] = x.astype(o_ref.dtype)
        return carry

    lax.fori_loop(0, NA_ROWS_PER_STEP // NA_ROWS_PER_ITER, rows_body, 0)


def _na_bias_table(rpb):
    pat = np.arange(NA_WIN_ROWS)[:, None]
    w = np.arange(NA_WIN_ROWS)[None, :]
    dr_idx = w - pat + NA_WIN_ROWS - 1
    qc = np.arange(GRID_W)[:, None]
    kc = np.arange(GRID_W)[None, :]
    win_start = np.clip(qc - NA_WIN_COLS // 2, 0, GRID_W - NA_WIN_COLS)
    ok = (kc >= win_start) & (kc < win_start + NA_WIN_COLS)
    dc_idx = np.clip(kc - qc + NA_WIN_COLS - 1, 0, NA_RPB_COLS - 1)
    t = rpb.astype(F32)[:, dr_idx][:, :, :, dc_idx]
    t = jnp.where(jnp.asarray(ok)[None, None, None], t, NA_MASK)
    t = jnp.transpose(t, (1, 0, 3, 2, 4))
    return t.reshape(NA_WIN_ROWS, NA_HEADS, GRID_W, NA_KEYS)


def na_attention(q, k, v, bias):
    b, _, L, _ = q.shape
    rows = L // GRID_W
    assert rows % NA_ROWS_PER_STEP == 0
    nblk = rows // NA_ROWS_PER_STEP
    assert nblk >= NA_WIN_BLOCKS
    cur = pl.BlockSpec((1, NA_HEADS, NA_BLOCK, NA_HEAD_DIM), lambda bi, i: (bi, 0, i, 0))
    window = pl.BlockSpec((pl.Element(1), pl.Element(NA_HEADS), pl.Element(NA_WIN_BLOCKS * NA_BLOCK),
                           pl.Element(NA_HEAD_DIM)),
                          lambda bi, i: (bi, 0, _na_window_block(i, nblk) * NA_BLOCK, 0))
    return pl.pallas_call(
        functools.partial(_na_attn_kernel, rows=rows),
        grid=(b, nblk),
        in_specs=[cur, window, window, _resident(bias)],
        out_specs=pl.BlockSpec((1, NA_BLOCK, NA_WIDTH), lambda bi, i: (bi, i, 0)),
        out_shape=jax.ShapeDtypeStruct((b, L, NA_WIDTH), BF16),
        compiler_params=_params("parallel", "parallel"),
        name="na_attn",
    )(q, k, v, bias)


def _rope_tables(L):
    t = np.arange(L)
    row = (t // GRID_W).astype(np.float32)
    col = (t % GRID_W).astype(np.float32)
    n_freq = MLA_ROPE // 4
    inv_freq = jnp.asarray(ROPE_THETA, F32) ** (-jnp.arange(n_freq, dtype=F32) / n_freq)
    ang = jnp.concatenate([jnp.asarray(row)[:, None] * inv_freq, jnp.asarray(col)[:, None] * inv_freq], axis=-1)
    cos, sin = jnp.cos(ang), jnp.sin(ang)
    ones = jnp.ones((L, MLA_NOPE), F32)
    zpad = jnp.zeros((L, MLA_HEAD_PAD - MLA_QK), F32)
    znope = jnp.zeros((L, MLA_NOPE), F32)
    c_tab = jnp.concatenate([ones, cos, cos, zpad], axis=-1)
    s_tab = jnp.concatenate([znope, sin, sin, zpad], axis=-1)
    return c_tab, s_tab


def _mla_prep_kernel(x_ref, gcq_ref, gckv_ref, wuq_ref, wuqr_ref, wuk_ref, wuv_ref, vones_ref, gq_ref, gk_ref,
                     c_ref, s_ref, q_ref, k_ref, v_ref):
    lo = MLA_Q_RANK + MLA_KV_RANK
    cq = x_ref[0, :, 0:MLA_Q_RANK].astype(F32)
    ckv = x_ref[0, :, MLA_Q_RANK:lo].astype(F32)
    kr = x_ref[0, :, lo:lo + MLA_HEAD_PAD].astype(F32)
    kr_rot = x_ref[0, :, lo + MLA_HEAD_PAD:].astype(F32)
    cqn = (cq * lax.rsqrt(jnp.mean(cq * cq, axis=-1, keepdims=True) + NORM_EPS) * gcq_ref[...]).astype(BF16)
    ckvn = (ckv * lax.rsqrt(jnp.mean(ckv * ckv, axis=-1, keepdims=True) + NORM_EPS) * gckv_ref[...]).astype(BF16)
    q_raw = jnp.dot(cqn, wuq_ref[...], preferred_element_type=F32)
    q_rot = jnp.dot(cqn, wuqr_ref[...], preferred_element_type=F32)
    k_raw = jnp.dot(ckvn, wuk_ref[...], preferred_element_type=F32)
    v_ref[0] = (jnp.dot(ckvn, wuv_ref[...], preferred_element_type=F32) + vones_ref[...]).astype(BF16)
    s = s_ref[...]
    gqc = gq_ref[...] * c_ref[...]
    gkc = gk_ref[...] * c_ref[...]
    kr_rot_s = kr_rot * s

    def norm_rope(t, gc, rot_s):
        inv = lax.rsqrt(jnp.sum(t * t, axis=-1, keepdims=True) * (1.0 / MLA_QK) + NORM_EPS)
        return (inv * (t * gc + rot_s)).astype(BF16)

    for h in range(MLA_HEADS):
        sl = slice(h * MLA_HEAD_PAD, (h + 1) * MLA_HEAD_PAD)
        q_ref[0, :, sl] = norm_rope(q_raw[:, sl], gqc, q_rot[:, sl] * s)
        k_ref[0, :, sl] = norm_rope(k_raw[:, sl] + kr, gkc, kr_rot_s)


def mla_prep(mla_in, lw, tabs, b, L):
    tm = _tile(L, 512)
    width = MLA_HEADS * MLA_HEAD_PAD
    in_w = mla_in.shape[-1]
    x3 = mla_in.reshape(b, L, in_w)
    tab_spec = pl.BlockSpec((tm, MLA_HEAD_PAD), lambda i, j: (j, 0))
    weights = (lw["gcq"], lw["gckv"], lw["wuq"], lw["wuq_rot"], lw["wuk"], lw["wuv"], lw["v_ones"], lw["gq"], lw["gk"])
    slab_spec = pl.BlockSpec((1, tm, width), lambda i, j: (i, j, 0))
    slab = jax.ShapeDtypeStruct((b, L, width), BF16)
    return pl.pallas_call(
        _mla_prep_kernel,
        grid=(b, L // tm),
        in_specs=[pl.BlockSpec((1, tm, in_w), lambda i, j: (i, j, 0))]
        + [_resident(a) for a in weights] + [tab_spec] * 2,
        out_specs=[slab_spec, slab_spec, slab_spec],
        out_shape=[slab, slab, slab],
        compiler_params=_params("parallel", "parallel"),
        name="mla_prep",
    )(x3, *weights, *tabs)


def _mla_flash_kernel(q_ref, k_ref, v_ref, o_ref, m_ref, acc_ref):
    ki = pl.program_id(2)
    tk = k_ref.shape[1]

    @pl.when(ki == 0)
    def _():
        m_ref[...] = jnp.full(m_ref.shape, -jnp.inf, F32)
        acc_ref[...] = jnp.zeros(acc_ref.shape, F32)

    def scores(h):
        sl = slice(h * MLA_HEAD_PAD, (h + 1) * MLA_HEAD_PAD)
        return lax.dot_general(q_ref[0, :, sl], k_ref[0, :, sl], (((1,), (1,)), ((), ())),
                               preferred_element_type=F32)

    s_next = scores(0)
    for h in range(MLA_HEADS):
        sl = slice(h * MLA_HEAD_PAD, (h + 1) * MLA_HEAD_PAD)
        s = s_next
        if h + 1 < MLA_HEADS:
            s_next = scores(h + 1)
        m_prev = m_ref[h]
        m_next = jnp.maximum(m_prev, jnp.max(s, axis=-1, keepdims=True))
        alpha = jnp.exp2(m_prev - m_next)
        p = jnp.exp2(s - jnp.concatenate([m_next] * (tk // LANES), axis=1))
        acc_ref[h] = alpha * acc_ref[h] + jnp.dot(p.astype(BF16), v_ref[0, :, sl], preferred_element_type=F32)
        m_ref[h] = m_next

    @pl.when(ki == pl.num_programs(2) - 1)
    def _():
        lane = lax.broadcasted_iota(jnp.int32, (acc_ref.shape[1], LANES), 1)

        def normalised(h):
            acc = acc_ref[h]
            return acc / pltpu.roll(acc, MLA_V, 1)

        for hp in range(MLA_HEADS // 2):
            pair = jnp.where(lane < MLA_V, normalised(2 * hp), pltpu.roll(normalised(2 * hp + 1), MLA_V, 1))
            o_ref[0, :, hp * LANES:(hp + 1) * LANES] = pair.astype(o_ref.dtype)


def mla_flash(q, k, v, tq_pref=1024, tk_pref=2048):
    b, L, width = q.shape
    tq = _tile(L, tq_pref)
    tk = _tile(L, tk_pref, LANES)
    return pl.pallas_call(
        _mla_flash_kernel,
        grid=(b, L // tq, L // tk),
        in_specs=[pl.BlockSpec((1, tq, width), lambda bi, qi, ki: (bi, qi, 0)),
                  pl.BlockSpec((1, tk, width), lambda bi, qi, ki: (bi, ki, 0)),
                  pl.BlockSpec((1, tk, width), lambda bi, qi, ki: (bi, ki, 0))],
        out_specs=pl.BlockSpec((1, tq, MLA_WIDTH), lambda bi, qi, ki: (bi, qi, 0)),
        out_shape=jax.ShapeDtypeStruct((b, L, MLA_WIDTH), BF16),
        scratch_shapes=[pltpu.VMEM((MLA_HEADS, tq, LANES), F32),
                        pltpu.VMEM((MLA_HEADS, tq, LANES), F32)],
        compiler_params=_params("parallel", "parallel", "arbitrary"),
        name="mla_flash",
    )(q, k, v)


def _rope_partner(w, g):
    half = MLA_ROPE // 2
    wg = w.astype(F32) * g
    return jnp.concatenate([-wg[..., half:], wg[..., :half]], axis=-1)


def _mla_layer_weights(cq_norm, ckv_norm, w_uq, w_ukv, q_norm, k_norm):
    pad = MLA_HEAD_PAD - MLA_QK
    gq = q_norm.astype(F32) * (MLA_QK ** -0.5 * np.log2(np.e))
    wuq3 = w_uq.reshape(MLA_Q_RANK, MLA_HEADS, MLA_QK)
    wuq = jnp.pad(wuq3, ((0, 0), (0, 0), (0, pad)))
    wuq_rot = jnp.pad(_rope_partner(wuq3[:, :, MLA_NOPE:], gq[MLA_NOPE:]), ((0, 0), (0, 0), (MLA_NOPE, pad)))
    wukv = w_ukv.reshape(MLA_KV_RANK, MLA_HEADS, MLA_NOPE + MLA_V)
    wuk = jnp.pad(wukv[:, :, :MLA_NOPE], ((0, 0), (0, 0), (0, MLA_HEAD_PAD - MLA_NOPE)))
    wuv = jnp.pad(wukv[:, :, MLA_NOPE:], ((0, 0), (0, 0), (0, MLA_HEAD_PAD - MLA_V)))
    v_ones = np.tile(np.arange(MLA_HEAD_PAD) >= MLA_V, MLA_HEADS).astype(np.float32)
    return {
        "v_ones": jnp.asarray(v_ones).reshape(1, -1),
        "gcq": cq_norm.reshape(1, -1).astype(F32),
        "gckv": ckv_norm.reshape(1, -1).astype(F32),
        "wuq": wuq.reshape(MLA_Q_RANK, -1).astype(BF16),
        "wuq_rot": wuq_rot.reshape(MLA_Q_RANK, -1).astype(BF16),
        "wuk": wuk.reshape(MLA_KV_RANK, -1).astype(BF16),
        "wuv": wuv.reshape(MLA_KV_RANK, -1).astype(BF16),
        "gq": jnp.pad(gq, (0, pad)).reshape(1, -1),
        "gk": jnp.pad(k_norm.astype(F32), (0, pad)).reshape(1, -1),
    }


RW_CHUNK = 64
RW_SUB = 16
RW_QUAD = 4
RW_QW = RW_QUAD * RW_HEAD_DIM
RW_SPLITS = (RW_WIDTH, 2 * RW_WIDTH, 3 * RW_WIDTH, 3 * RW_WIDTH + 2 * RW_DECAY_RANK,
             3 * RW_WIDTH + 2 * RW_DECAY_RANK + 2 * RW_A_RANK)
HALO = 8


def _dot3(x, w_hi, w_lo):
    hi, lo = _split2(x)
    return (jnp.dot(hi, w_hi, preferred_element_type=F32) + jnp.dot(lo, w_hi, preferred_element_type=F32)
            + jnp.dot(hi, w_lo, preferred_element_type=F32))


def _chunk_cumsum(x, tri):
    hi = x.astype(BF16)
    r1 = x - hi.astype(F32)
    mid = r1.astype(BF16)
    lo = (r1 - mid.astype(F32)).astype(BF16)
    return (jnp.dot(tri, hi, preferred_element_type=F32) + jnp.dot(tri, mid, preferred_element_type=F32)
            + jnp.dot(tri, lo, preferred_element_type=F32))


def _rw_prep(p, prev_row, next_row, mu_ref, wup_hi, wup_lo, w0_ref, aup_hi, aup_lo, a0_ref,
             gup_hi, gup_lo, kk_ref, ka_ref, rk_ref, ones_ref, trif_ref, trib_ref,
             r_ref, v_ref, kkn_ref, g_ref, bv_ref, lw_ref, kd_ref, a_ref, cum_ref):
    tm = p.shape[0]
    row = lax.broadcasted_iota(jnp.int32, p.shape, 0)
    prev = jnp.where(row == 0, prev_row, pltpu.roll(p, 1, 0))
    nxt = jnp.where(row == tm - 1, next_row, pltpu.roll(p, tm - 1, 0))
    pm = p + mu_ref[...] * (0.5 * (prev + nxt) - p)
    s0, s1, s2, s3, s4 = RW_SPLITS
    r, k, v = pm[:, 0:s0], pm[:, s0:s1], pm[:, s1:s2]
    wd = jnp.tanh(pm[:, s2:s3])
    ad = pm[:, s3:s4]
    gd = pm[:, s4:]
    w_raw = w0_ref[...] + _dot3(wd, wup_hi[...], wup_lo[...])
    lw = -np.float32(np.exp(-0.5)) * jax.nn.sigmoid(w_raw)
    a = jax.nn.sigmoid(a0_ref[...] + _dot3(ad, aup_hi[...], aup_lo[...]))
    g_ref[...] = _dot3(jax.nn.sigmoid(gd), gup_hi[...], gup_lo[...])
    kk = k * kk_ref[...]
    kkn_ref[...] = kk * lax.rsqrt(_seg_sum(kk * kk, ones_ref[...]) + 1e-12)
    kd_sum = None
    for d, tri_ref in enumerate((trif_ref, trib_ref)):
        sl = slice(d * RW_WIDTH, (d + 1) * RW_WIDTH)
        lw_ref[d] = lw[:, sl]
        cum_ref[d] = _chunk_cumsum(lw[:, sl], tri_ref[...])
        a_ref[d] = a[:, sl]
        kd = k * (1.0 + (a[:, sl] - 1.0) * ka_ref[...])
        kd_ref[d] = kd
        kd_sum = kd if kd_sum is None else kd_sum + kd
    r_ref[...] = r
    v_ref[...] = v
    bv_ref[...] = _seg_sum(r * kd_sum * rk_ref[...], ones_ref[...]) * v


N_RW_PREP_WEIGHTS = 15


def _in_proj_rw_kernel(x_ref, xp_ref, xn_ref, g_ref, wna_ref, wmla_ref, wgate_ref, wrw_ref, *refs, tiles_per_seq):
    rw_weights = refs[:N_RW_PREP_WEIGHTS]
    ona_ref, omla_ref, ogate_ref = refs[N_RW_PREP_WEIGHTS:N_RW_PREP_WEIGHTS + 3]
    rw_outs = refs[N_RW_PREP_WEIGHTS + 3:]
    pos = pl.program_id(0) % tiles_per_seq

    def normed(x):
        return (x * lax.rsqrt(jnp.mean(x * x, axis=-1, keepdims=True) + NORM_EPS) * g_ref[...]).astype(BF16)

    h = normed(x_ref[...])
    ona_ref[...] = jnp.dot(h, wna_ref[...], preferred_element_type=F32).astype(ona_ref.dtype)
    omla_ref[...] = jnp.dot(h, wmla_ref[...], preferred_element_type=F32).astype(omla_ref.dtype)
    ogate_ref[...] = jnp.dot(h, wgate_ref[...], preferred_element_type=F32).astype(ogate_ref.dtype)
    p = jnp.dot(h, wrw_ref[...], preferred_element_type=F32)
    halo = normed(jnp.concatenate([xp_ref[...], xn_ref[...]], axis=0))
    p_halo = jnp.dot(halo, wrw_ref[...], preferred_element_type=F32)
    prev_row = jnp.where(pos > 0, p_halo[HALO - 1:HALO], 0.0)
    next_row = jnp.where(pos < tiles_per_seq - 1, p_halo[HALO:HALO + 1], 0.0)
    _rw_prep(p, prev_row, next_row, *rw_weights, *rw_outs)


def in_proj_rw(x, g, w_na, w_mla, w_gate, w_rw, lw, L):
    t, d = x.shape
    tm = _tile(L, 256, RW_CHUNK)
    nh = t // HALO
    g = g.reshape(1, d)
    t_i = np.arange(tm)[:, None]
    s_i = np.arange(tm)[None, :]
    same_chunk = (t_i // RW_CHUNK) == (s_i // RW_CHUNK)
    tri_f = jnp.asarray(same_chunk & (s_i <= t_i), BF16)
    tri_b = jnp.asarray(same_chunk & (s_i >= t_i), BF16)
    rw_weights = (lw["mu"], lw["wup_hi"], lw["wup_lo"], lw["w0"], lw["aup_hi"], lw["aup_lo"], lw["a0"],
                  lw["gup_hi"], lw["gup_lo"], lw["k_k"], lw["k_a"], lw["r_k"], lw["ones_bd"], tri_f, tri_b)
    assert len(rw_weights) == N_RW_PREP_WEIGHTS
    proj = (w_na, w_mla, w_gate)
    one = jax.ShapeDtypeStruct((t, RW_WIDTH), F32)
    two = jax.ShapeDtypeStruct((2, t, RW_WIDTH), F32)
    one_spec = pl.BlockSpec((tm, RW_WIDTH), lambda i: (i, 0))
    two_spec = pl.BlockSpec((2, tm, RW_WIDTH), lambda i: (0, i, 0))
    return pl.pallas_call(
        functools.partial(_in_proj_rw_kernel, tiles_per_seq=L // tm),
        grid=(t // tm,),
        in_specs=[pl.BlockSpec((tm, d), lambda i: (i, 0)),
                  pl.BlockSpec((HALO, d), lambda i: (jnp.maximum(i * (tm // HALO) - 1, 0), 0)),
                  pl.BlockSpec((HALO, d), lambda i: (jnp.minimum((i + 1) * (tm // HALO), nh - 1), 0)),
                  _resident(g)] + [_resident(w) for w in proj + (w_rw,) + rw_weights],
        out_specs=[pl.BlockSpec((tm, w.shape[1]), lambda i: (i, 0)) for w in proj]
        + [one_spec] * 5 + [two_spec] * 4,
        out_shape=[jax.ShapeDtypeStruct((t, w.shape[1]), BF16) for w in proj] + [one] * 5 + [two] * 4,
        compiler_params=_params("parallel"),
        name="in_proj_rw",
    )(x, x, x, g, *proj, w_rw, *rw_weights)


def _rw_masks(reverse):
    C, NQ = RW_CHUNK, RW_QUAD
    wt = lax.broadcasted_iota(jnp.int32, (C, NQ * C), 0)
    ws = lax.broadcasted_iota(jnp.int32, (C, NQ * C), 1) & (C - 1)
    strict = (ws > wt) if reverse else (ws < wt)
    incl = (ws >= wt) if reverse else (ws <= wt)
    same = (ws // RW_SUB) == (wt // RW_SUB)
    eye = jnp.where(ws == wt, 1.0, 0.0)
    return strict, incl, same, eye


def _rw_chunks(chains, bdm, bdm_b):
    C, NQ = RW_CHUNK, RW_QUAD
    n = len(chains)
    masks = {rev: _rw_masks(rev) for rev in sorted({c["reverse"] for c in chains})}
    strict = [masks[c["reverse"]][0] for c in chains]
    incl = [masks[c["reverse"]][1] for c in chains]
    same = [masks[c["reverse"]][2] for c in chains]
    eye = [masks[c["reverse"]][3] for c in chains]
    ids = range(n)

    def dot(a, b):
        return jnp.dot(a, b, preferred_element_type=F32)

    def dot_nt(a, b):
        return lax.dot_general(a, b, (((1,), (1,)), ((), ())), preferred_element_type=F32)

    def bd(x):
        return jnp.concatenate([x] * NQ, axis=0) * bdm_b

    def mm(ms, xs):
        return [dot(m.astype(BF16), bd(x.astype(BF16))) for m, x in zip(ms, xs)]

    lw = [c["lw"] for c in chains]
    cum = [c["cum"] for c in chains]
    tot =[cum[i][0:1] if chains[i]["reverse"] else cum[i][C - 1:C] for i in ids]
    pinv = [jnp.exp(-x) for x in cum]
    pend = [jnp.exp(t - x) for t, x in zip(tot, cum)]
    kka = [c["kk"] * c["a"] for c in chains]
    kap = [(chains[i]["kk"] * jnp.exp(cum[i] - lw[i])).astype(BF16) for i in ids]
    bet = [(kka[i] * pinv[i]).astype(BF16) for i in ids]
    kt = [(chains[i]["kd"] * pinv[i]).astype(BF16) for i in ids]
    rt = [(chains[i]["r"] * jnp.exp(cum[i])).astype(BF16) for i in ids]
    v = [c["v"] for c in chains]

    x2 = [jnp.concatenate([kap[i], rt[i]], axis=0) for i in ids]
    y2 = [jnp.concatenate([bd(bet[i]), bd(kt[i])], axis=0) for i in ids]
    aw = [dot_nt(x2[i], y2[i]) for i in ids]
    zz = [dot_nt(x2[i], chains[i]["zt"].astype(BF16)) for i in ids]
    a_ab = [jnp.where(strict[i], aw[i][:C, :NQ * C], 0.0) for i in ids]
    a_ak = [jnp.where(strict[i], aw[i][:C, NQ * C:], 0.0) for i in ids]
    a_rb = [jnp.where(incl[i], aw[i][C:, :NQ * C], 0.0) for i in ids]
    a_rk = [jnp.where(incl[i], aw[i][C:, NQ * C:], 0.0) for i in ids]
    def mm2(tops, bottoms, xs):
        both = mm([jnp.concatenate([a, b], axis=0) for a, b in zip(tops, bottoms)], xs)
        return [x[:C] for x in both], [x[C:] for x in both]

    akv, yv = mm2(a_ak, a_rk, v)
    rhs = [-(zz[i][:C] + akv[i]) for i in ids]
    d = [jnp.where(same[i], a_ab[i], 0.0) for i in ids]
    e = [a_ab[i] - d[i] for i in ids]
    d2 = mm(d, d)
    t = [eye[i] - d[i] for i in ids]
    d4, td = mm2(d2, t, d2)
    t = [x + y for x, y in zip(t, td)]
    d8, td = mm2(d4, t, d4)
    t = [x + y for x, y in zip(t, td)]
    t = [x + y for x, y in zip(t, mm(t, d8))]
    nn = mm(t, e)
    n2 = mm(nn, nn)
    u = mm(t, rhs)
    u = [x + y for x, y in zip(u, mm(n2, u))]
    u = [x - y for x, y in zip(u, mm(nn, u))]
    yu = mm(a_rb, u)
    y = [zz[i][C:] + yv[i] + yu[i] for i in ids]
    lhs = [jnp.concatenate([v[i], u[i]], axis=0).astype(BF16) for i in ids]
    rhs2 = [jnp.concatenate([chains[i]["kd"] * pend[i], kka[i] * pend[i]], axis=0).astype(BF16) for i in ids]
    upd = [lax.dot_general(lhs[i], rhs2[i], (((0,), (0,)), ((), ())), preferred_element_type=F32) for i in ids]
    zt = [chains[i]["zt"] * jnp.exp(tot[i]) + upd[i] * bdm for i in ids]
    return list(zip(y, zt))


RW_SEQS = 4


def _rw_scan_kernel(rf, vf, kkf, lwf, kdf, af, cf, rb, vb, kkb, lwb, kdb, ab, cb, bdm_ref, yf_ref, yb_ref, zt_ref):
    @pl.when(pl.program_id(1) == 0)
    def _():
        zt_ref[...] = jnp.zeros(zt_ref.shape, F32)

    bdm = bdm_ref[...]
    nq = RW_HEADS // RW_QUAD
    names = ("r", "lw", "kd", "v", "kk", "a", "cum")
    chains, outs = [], []
    for bi in range(RW_SEQS):
        for di, (reverse, refs, y_ref) in enumerate(((False, (rf, lwf, kdf, vf, kkf, af, cf), yf_ref),
                                                     (True, (rb, lwb, kdb, vb, kkb, ab, cb), yb_ref))):
            for q in range(nq):
                sl = slice(q * RW_QW, (q + 1) * RW_QW)
                idx = (bi * 2 + di) * nq + q
                chain = {name: ref[bi, :, sl] for name, ref in zip(names, refs)}
                chain["zt"] = zt_ref[idx]
                chain["reverse"] = reverse
                chains.append(chain)
                outs.append((y_ref, bi, sl, idx))
    for (y, zt), (y_ref, bi, sl, idx) in zip(_rw_chunks(chains, bdm, bdm.astype(BF16)), outs):
        y_ref[bi, :, sl] = y
        zt_ref[idx] = zt


def rw_scan(r, v, kk, lw, kd, a, cum, bdm):
    b, L, _ = r.shape
    assert b % RW_SEQS == 0
    nc = L // RW_CHUNK
    blk = (RW_SEQS, RW_CHUNK, RW_WIDTH)
    f1 = pl.BlockSpec(blk, lambda bi, c: (bi, c, 0))
    b1 = pl.BlockSpec(blk, lambda bi, c: (bi, nc - 1 - c, 0))
    f2 = pl.BlockSpec((None,) + blk, lambda bi, c: (0, bi, c, 0))
    b2 = pl.BlockSpec((None,) + blk, lambda bi, c: (1, bi, nc - 1 - c, 0))
    out = jax.ShapeDtypeStruct((b, L, RW_WIDTH), F32)
    return pl.pallas_call(
        _rw_scan_kernel,
        grid=(b // RW_SEQS, nc),
        in_specs=[f1, f1, f1, f2, f2, f2, f2, b1, b1, b1, b2, b2, b2, b2, _resident(bdm)],
        out_specs=[f1, b1],
        out_shape=[out, out],
        scratch_shapes=[pltpu.VMEM((RW_SEQS * 2 * RW_HEADS // RW_QUAD, RW_QW, RW_QW), F32)],
        compiler_params=_params("parallel", "arbitrary"),
        name="rw_scan",
    )(r, v, kk, lw, kd, a, cum, r, v, kk, lw, kd, a, cum, bdm)


def _rw_post_kernel(yf_ref, yb_ref, bv_ref, g_ref, lnw_ref, lnb_ref, ones_ref, o_ref):
    ones_bd = ones_ref[...]
    inv_n = 1.0 / RW_HEAD_DIM
    y = yf_ref[...] + yb_ref[...]
    yc = y - _seg_sum(y, ones_bd) * inv_n
    var = _seg_sum(yc * yc, ones_bd) * inv_n
    y = yc * lax.rsqrt(var + RW_LN_EPS) * lnw_ref[...] + lnb_ref[...]
    o_ref[...] = ((y + bv_ref[...]) * g_ref[...]).astype(o_ref.dtype)


def rw_post(yf, yb, bv, g, lw):
    t = yf.shape[0]
    tm = _tile(t, 512)
    one = pl.BlockSpec((tm, RW_WIDTH), lambda i: (i, 0))
    weights = (lw["ln_w"], lw["ln_b"], lw["ones_bd"])
    return pl.pallas_call(
        _rw_post_kernel,
        grid=(t // tm,),
        in_specs=[one, one, one, one] + [_resident(a) for a in weights],
        out_specs=one,
        out_shape=jax.ShapeDtypeStruct((t, RW_WIDTH), BF16),
        compiler_params=_params("parallel"),
        name="rw_post",
    )(yf, yb, bv, g, *weights)


def _block_diag2(w):
    z = jnp.zeros_like(w[0])
    return jnp.concatenate([jnp.concatenate([w[0], z], axis=1), jnp.concatenate([z, w[1]], axis=1)], axis=0)


def _hi_lo(w):
    w = w.astype(F32)
    hi = w.astype(BF16)
    return hi, (w - hi.astype(F32)).astype(BF16)


def _rw_layer_weights(mu, w0, w_up, a0, a_up, g_up, k_k, k_a, r_k, ln_w, ln_b):
    wup_hi, wup_lo = _hi_lo(_block_diag2(w_up))
    aup_hi, aup_lo = _hi_lo(_block_diag2(a_up))
    gup_hi, gup_lo = _hi_lo(g_up)

    def row(t):
        return t.reshape(1, -1).astype(F32)

    return {
        "mu": row(mu), "wup_hi": wup_hi, "wup_lo": wup_lo, "w0": row(w0),
        "aup_hi": aup_hi, "aup_lo": aup_lo, "a0": row(a0), "gup_hi": gup_hi, "gup_lo": gup_lo,
        "k_k": row(k_k), "k_a": row(k_a), "r_k": row(r_k), "ln_w": row(ln_w), "ln_b": row(ln_b),
        "ones_bd": _head_ones(RW_WIDTH, RW_HEAD_DIM),
    }


def rwkv7_mix(prepared, lw, bdm, b, L):
    r, v, kk, g, bv, lwd, kd, a, cum = prepared
    t = b * L

    def seq(x):
        return x.reshape(x.shape[:-2] + (b, L, RW_WIDTH))

    yf, yb = rw_scan(seq(r), seq(v), seq(kk), seq(lwd), seq(kd), seq(a), seq(cum), bdm)
    return rw_post(yf.reshape(t, RW_WIDTH), yb.reshape(t, RW_WIDTH), bv, g, lw)


def _merge_kernel(x_ref, ya_ref, yb_ref, yc_ref, gin_ref, bg_ref, pa_ref, pb_ref, pc_ref, wo_ref, o_ref):
    mixed = None
    for i, (y_ref, p_ref) in enumerate(((ya_ref, pa_ref), (yb_ref, pb_ref), (yc_ref, pc_ref))):
        sl = slice(i * D_MODEL, (i + 1) * D_MODEL)
        gate = jax.nn.sigmoid(gin_ref[:, sl].astype(F32) + bg_ref[:, sl])
        term = gate * jnp.dot(y_ref[...], p_ref[...], preferred_element_type=F32)
        mixed = term if mixed is None else mixed + term
    o_ref[...] = x_ref[...] + jnp.dot(mixed.astype(BF16), wo_ref[...], preferred_element_type=F32)


def merge(x, ya, yb, yc, gate_in, b_gate, pa, pb, pc, wo):
    t = x.shape[0]
    tm = _tile(t, 512)

    def rows(w):
        return pl.BlockSpec((tm, w), lambda i: (i, 0))

    def full(a):
        return pl.BlockSpec(a.shape, lambda i: (0,) * a.ndim)

    return pl.pallas_call(
        _merge_kernel,
        grid=(t // tm,),
        in_specs=[rows(D_MODEL), rows(NA_WIDTH), rows(MLA_WIDTH), rows(RW_WIDTH), rows(N_BRANCH * D_MODEL),
                  full(b_gate), full(pa), full(pb), full(pc), full(wo)],
        out_specs=rows(D_MODEL),
        out_shape=jax.ShapeDtypeStruct((t, D_MODEL), F32),
        compiler_params=_params("parallel"),
        name="merge",
    )(x, ya, yb, yc, gate_in, b_gate, pa, pb, pc, wo)


def _ffn_kernel(x_ref, g_ref, wg_ref, wu_ref, wd_ref, o_ref):
    x = x_ref[...]
    ms = jnp.mean(x * x, axis=-1, keepdims=True)
    h = (x * lax.rsqrt(ms + NORM_EPS) * g_ref[...]).astype(BF16)
    gate = jnp.dot(h, wg_ref[...], preferred_element_type=F32)
    up = jnp.dot(h, wu_ref[...], preferred_element_type=F32)
    act = (gate * jax.nn.sigmoid(gate) * up).astype(BF16)
    o_ref[...] = x + jnp.dot(act, wd_ref[...], preferred_element_type=F32)


def ffn(x, g, wg, wu, wd, tm_pref=512):
    t, d = x.shape
    tm = _tile(t, tm_pref)
    g = g.reshape(1, d)
    return pl.pallas_call(
        _ffn_kernel,
        grid=(t // tm,),
        in_specs=[pl.BlockSpec((tm, d), lambda i: (i, 0))] + [_resident(a) for a in (g, wg, wu, wd)],
        out_specs=pl.BlockSpec((tm, d), lambda i: (i, 0)),
        out_shape=jax.ShapeDtypeStruct((t, d), F32),
        compiler_params=_params("parallel"),
        name="ffn",
    )(x, g, wg, wu, wd)


IN_SIZES = (NA_WIDTH, NA_WIDTH, NA_WIDTH, MLA_Q_RANK, MLA_KV_RANK, MLA_ROPE, RW_IN, N_BRANCH * D_MODEL)
IN_SPLITS = tuple(int(s) for s in np.cumsum(IN_SIZES)[:-1])


def _layer_weights(l, norm1_g, w_in, b_gate, na_q_norm, na_k_norm, na_rpb, na_proj,
                   mla_cq_norm, mla_ckv_norm, mla_w_uq, mla_w_ukv, mla_q_norm, mla_k_norm, mla_proj,
                   rw_mu, rw_w0, rw_w_up, rw_a0, rw_a_up, rw_g_up, rw_k_k, rw_k_a, rw_r_k, rw_ln_w, rw_ln_b, rw_proj,
                   w_out, norm2_g, ffn_w_gate, ffn_w_up, ffn_w_down):
    w = w_in[l]
    slab_pad = ((0, 0), (MLA_NOPE, MLA_HEAD_PAD - MLA_QK))
    w_kr = w[:, IN_SPLITS[4]:IN_SPLITS[5]]
    w_kr_rot = _rope_partner(w_kr, mla_k_norm[l].astype(F32)[MLA_NOPE:])
    w_mla = jnp.concatenate([w[:, IN_SPLITS[2]:IN_SPLITS[4]], jnp.pad(w_kr, slab_pad), jnp.pad(w_kr_rot, slab_pad)],
                            axis=1)
    return {
        "norm1_g": norm1_g[l], "w_na": w[:, :IN_SPLITS[2]].astype(BF16), "w_mla": w_mla.astype(BF16),
        "w_rw": w[:, IN_SPLITS[5]:IN_SPLITS[6]].astype(BF16),
        "w_gate": w[:, IN_SPLITS[6]:].astype(BF16),
        "b_gate": b_gate[l].reshape(1, -1).astype(F32),
        "na_gq": na_q_norm[l].astype(F32) * (NA_HEAD_DIM ** -0.5), "na_gk": na_k_norm[l].astype(F32),
        "na_bias": _na_bias_table(na_rpb[l]),
        "na_proj": na_proj[l].astype(BF16), "mla_proj": mla_proj[l].astype(BF16), "rw_proj": rw_proj[l].astype(BF16),
        "mla": _mla_layer_weights(mla_cq_norm[l], mla_ckv_norm[l], mla_w_uq[l], mla_w_ukv[l], mla_q_norm[l], mla_k_norm[l]),
        "rw": _rw_layer_weights(rw_mu[l], rw_w0[l], rw_w_up[l], rw_a0[l], rw_a_up[l], rw_g_up[l],
                                rw_k_k[l], rw_k_a[l], rw_r_k[l], rw_ln_w[l], rw_ln_b[l]),
        "w_out": w_out[l].astype(BF16), "norm2_g": norm2_g[l],
        "ffn_w_gate": ffn_w_gate[l].astype(BF16), "ffn_w_up": ffn_w_up[l].astype(BF16),
        "ffn_w_down": ffn_w_down[l].astype(BF16),
    }


def _trunk(x, layers):
    b, L, d = x.shape
    t = b * L
    x = x.reshape(t, d)
    tabs = _rope_tables(L)
    bdm = _head_ones(RW_QW, RW_HEAD_DIM).astype(F32)
    for lw in layers:
        na_in, mla_in, gate_in, *rw_prepared = in_proj_rw(x, lw["norm1_g"], lw["w_na"], lw["w_mla"], lw["w_gate"],
                                                          lw["w_rw"], lw["rw"], L)
        qa, ka, va = na_prep(na_in, lw["na_gq"], lw["na_gk"], b, L)
        y_a = na_attention(qa, ka, va, lw["na_bias"]).reshape(t, NA_WIDTH)
        qm, km, vm = mla_prep(mla_in, lw["mla"], tabs, b, L)
        y_b = mla_flash(qm, km, vm).reshape(t, MLA_WIDTH)
        y_c = rwkv7_mix(rw_prepared, lw["rw"], bdm, b, L)
        x = merge(x, y_a, y_b, y_c, gate_in, lw["b_gate"], lw["na_proj"], lw["mla_proj"], lw["rw_proj"], lw["w_out"])
        x = ffn(x, lw["norm2_g"], lw["ffn_w_gate"], lw["ffn_w_up"], lw["ffn_w_down"])
    return x.reshape(b, L, d)


def kernel(x_prompt, x_sample, norm1_g, w_in, b_gate, na_q_norm, na_k_norm, na_rpb, na_proj, mla_cq_norm, mla_ckv_norm, mla_w_uq, mla_w_ukv, mla_q_norm, mla_k_norm, mla_proj, rw_mu, rw_w0, rw_w_up, rw_a0, rw_a_up, rw_g_up, rw_k_k, rw_k_a, rw_r_k, rw_ln_w, rw_ln_b, rw_proj, w_out, norm2_g, ffn_w_gate, ffn_w_up, ffn_w_down):
    weights = (norm1_g, w_in, b_gate, na_q_norm, na_k_norm, na_rpb, na_proj,
               mla_cq_norm, mla_ckv_norm, mla_w_uq, mla_w_ukv, mla_q_norm, mla_k_norm, mla_proj,
               rw_mu, rw_w0, rw_w_up, rw_a0, rw_a_up, rw_g_up, rw_k_k, rw_k_a, rw_r_k, rw_ln_w, rw_ln_b, rw_proj,
               w_out, norm2_g, ffn_w_gate, ffn_w_up, ffn_w_down)
    layers = [_layer_weights(l, *weights) for l in range(norm1_g.shape[0])]
    return (_trunk(x_prompt, layers), _trunk(x_sample, layers))
```

```python
import functools

import jax
import jax.numpy as jnp
import numpy as np
from jax import lax
from jax.experimental import pallas as pl
from jax.experimental.pallas import tpu as pltpu

F32 = jnp.float32
BF16 = jnp.bfloat16

D_MODEL = 1024
GRID_W = 64
N_BRANCH = 3
NORM_EPS = 1e-6

NA_HEADS = 8
NA_HEAD_DIM = 64
NA_WIDTH = NA_HEADS * NA_HEAD_DIM
NA_WIN_ROWS = 8
NA_WIN_COLS = 16
NA_RPB_ROWS = 2 * NA_WIN_ROWS - 1
NA_RPB_COLS = 2 * NA_WIN_COLS - 1
NA_MASK = -1e30

MLA_HEADS = 8
MLA_NOPE = 64
MLA_ROPE = 32
MLA_QK = MLA_NOPE + MLA_ROPE
MLA_V = 64
MLA_WIDTH = MLA_HEADS * MLA_V
MLA_Q_RANK = 256
MLA_KV_RANK = 128
MLA_HEAD_PAD = 128
ROPE_THETA = 10000.0

RW_HEADS = 8
RW_HEAD_DIM = 64
RW_WIDTH = RW_HEADS * RW_HEAD_DIM
RW_DECAY_RANK = 64
RW_A_RANK = 64
RW_G_RANK = 128
RW_LN_EPS = 64e-5
RW_IN = 3 * RW_WIDTH + 2 * RW_DECAY_RANK + 2 * RW_A_RANK + RW_G_RANK

D_FF = 2816

VMEM_LIMIT_BYTES = 56 * 1024 * 1024
LANES = 128


def _params(*sem):
    return pltpu.CompilerParams(dimension_semantics=sem, vmem_limit_bytes=VMEM_LIMIT_BYTES)


def _tile(n, pref, mult=8):
    if n <= pref:
        return n
    t = (pref // mult) * mult
    while t >= mult:
        if n % t == 0:
            return t
        t -= mult
    return n


def _resident(a):
    return pl.BlockSpec(a.shape, lambda *_: (0,) * a.ndim, pipeline_mode=pl.Buffered(1))


def _split2(x):
    hi = x.astype(BF16)
    return hi, (x - hi.astype(F32)).astype(BF16)


def _seg_sum(x, ones_bd):
    hi, lo = _split2(x)
    return jnp.dot(hi, ones_bd, preferred_element_type=F32) + jnp.dot(lo, ones_bd, preferred_element_type=F32)


def _head_ones(width, head_dim):
    head = np.arange(width) // head_dim
    return jnp.asarray(head[:, None] == head[None, :], BF16)


def _na_prep_kernel(x_ref, gq_ref, gk_ref, ones_ref, q_ref, k_ref, v_ref):
    inv_n = 1.0 / NA_HEAD_DIM

    def normed(t, g):
        return (t * lax.rsqrt(_seg_sum(t * t, ones_ref[...]) * inv_n + NORM_EPS) * g).astype(BF16)

    q = normed(x_ref[0, :, 0:NA_WIDTH].astype(F32), gq_ref[...])
    k = normed(x_ref[0, :, NA_WIDTH:2 * NA_WIDTH].astype(F32), gk_ref[...])
    for h in range(NA_HEADS):
        sl = slice(h * NA_HEAD_DIM, (h + 1) * NA_HEAD_DIM)
        q_ref[0, h] = q[:, sl]
        k_ref[0, h] = k[:, sl]
        v_ref[0, h] = x_ref[0, :, 2 * NA_WIDTH + sl.start:2 * NA_WIDTH + sl.stop]


def na_prep(na_in, gq, gk, b, L):
    tm = _tile(L, 512)
    x3 = na_in.reshape(b, L, 3 * NA_WIDTH)
    hm = jax.ShapeDtypeStruct((b, NA_HEADS, L, NA_HEAD_DIM), BF16)
    hm_spec = pl.BlockSpec((1, NA_HEADS, tm, NA_HEAD_DIM), lambda i, j: (i, 0, j, 0))
    consts = (jnp.tile(gq, NA_HEADS).reshape(1, -1), jnp.tile(gk, NA_HEADS).reshape(1, -1),
              _head_ones(NA_WIDTH, NA_HEAD_DIM))
    return pl.pallas_call(
        _na_prep_kernel,
        grid=(b, L // tm),
        in_specs=[pl.BlockSpec((1, tm, 3 * NA_WIDTH), lambda i, j: (i, j, 0))] + [_resident(a) for a in consts],
        out_specs=[hm_spec, hm_spec, hm_spec],
        out_shape=[hm, hm, hm],
        compiler_params=_params("parallel", "parallel"),
        name="na_prep",
    )(x3, *consts)


NA_ROWS_PER_STEP = 8
NA_WIN_BLOCKS = 3
NA_ROWS_PER_ITER = 4
NA_BLOCK = NA_ROWS_PER_STEP * GRID_W
NA_KEYS = NA_WIN_ROWS * GRID_W


def _na_window_block(i, nblk):
    return jnp.clip(i - 1, 0, nblk - NA_WIN_BLOCKS)


def _na_attn_kernel(q_ref, k_ref, v_ref, bias_ref, o_ref, *, rows):
    i = pl.program_id(1)
    base_row = _na_window_block(i, rows // NA_ROWS_PER_STEP) * NA_ROWS_PER_STEP

    def rows_body(j, carry):
        work = []
        for jj in range(NA_ROWS_PER_ITER):
            jr = j * NA_ROWS_PER_ITER + jj
            r = i * NA_ROWS_PER_STEP + jr
            rs = jnp.clip(r - NA_WIN_ROWS // 2, 0, rows - NA_WIN_ROWS)
            off = pl.multiple_of((rs - base_row) * GRID_W, GRID_W)
            qoff = pl.multiple_of(jr * GRID_W, GRID_W)
            work += [(h, off, r - rs, qoff) for h in range(NA_HEADS)]
        s = [lax.dot_general(q_ref[0, h, pl.ds(qoff, GRID_W), :], k_ref[0, h, pl.ds(off, NA_KEYS), :],
                             (((1,), (1,)), ((), ())), preferred_element_type=F32) + bias_ref[pat, h]
             for h, off, pat, qoff in work]
        p = [jnp.exp(x - jnp.max(x, axis=-1, keepdims=True)) for x in s]
        l = [jnp.sum(x, axis=-1, keepdims=True) for x in p]
        o = [jnp.dot(x.astype(BF16), v_ref[0, h, pl.ds(off, NA_KEYS), :], preferred_element_type=F32) / y
             for x, y, (h, off, _, _) in zip(p, l, work)]
        for x, (h, _, _, qoff) in zip(o, work):
            o_ref[0, pl.ds(qoff, GRID_W), h * NA_HEAD_DIM:(h + 1) * NA_HEAD_DIM] = x.astype(o_ref.dtype)
        return carry

    lax.fori_loop(0, NA_ROWS_PER_STEP // NA_ROWS_PER_ITER, rows_body, 0)


def _na_bias_table(rpb):
    pat = np.arange(NA_WIN_ROWS)[:, None]
    w = np.arange(NA_WIN_ROWS)[None, :]
    dr_idx = w - pat + NA_WIN_ROWS - 1
    qc = np.arange(GRID_W)[:, None]
    kc = np.arange(GRID_W)[None, :]
    win_start = np.clip(qc - NA_WIN_COLS // 2, 0, GRID_W - NA_WIN_COLS)
    ok = (kc >= win_start) & (kc < win_start + NA_WIN_COLS)
    dc_idx = np.clip(kc - qc + NA_WIN_COLS - 1, 0, NA_RPB_COLS - 1)
    t = rpb.astype(F32)[:, dr_idx][:, :, :, dc_idx]
    t = jnp.where(jnp.asarray(ok)[None, None, None], t, NA_MASK)
    t = jnp.transpose(t, (1, 0, 3, 2, 4))
    return t.reshape(NA_WIN_ROWS, NA_HEADS, GRID_W, NA_KEYS)


def na_attention(q, k, v, bias):
    b, _, L, _ = q.shape
    rows = L // GRID_W
    assert rows % NA_ROWS_PER_STEP == 0
    nblk = rows // NA_ROWS_PER_STEP
    assert nblk >= NA_WIN_BLOCKS
    cur = pl.BlockSpec((1, NA_HEADS, NA_BLOCK, NA_HEAD_DIM), lambda bi, i: (bi, 0, i, 0))
    window = pl.BlockSpec((pl.Element(1), pl.Element(NA_HEADS), pl.Element(NA_WIN_BLOCKS * NA_BLOCK),
                           pl.Element(NA_HEAD_DIM)),
                          lambda bi, i: (bi, 0, _na_window_block(i, nblk) * NA_BLOCK, 0))
    return pl.pallas_call(
        functools.partial(_na_attn_kernel, rows=rows),
        grid=(b, nblk),
        in_specs=[cur, window, window, _resident(bias)],
        out_specs=pl.BlockSpec((1, NA_BLOCK, NA_WIDTH), lambda bi, i: (bi, i, 0)),
        out_shape=jax.ShapeDtypeStruct((b, L, NA_WIDTH), BF16),
        compiler_params=_params("parallel", "parallel"),
        name="na_attn",
    )(q, k, v, bias)


def _rope_tables(L):
    t = np.arange(L)
    row = (t // GRID_W).astype(np.float32)
    col = (t % GRID_W).astype(np.float32)
    n_freq = MLA_ROPE // 4
    inv_freq = jnp.asarray(ROPE_THETA, F32) ** (-jnp.arange(n_freq, dtype=F32) / n_freq)
    ang = jnp.concatenate([jnp.asarray(row)[:, None] * inv_freq, jnp.asarray(col)[:, None] * inv_freq], axis=-1)
    cos, sin = jnp.cos(ang), jnp.sin(ang)
    ones = jnp.ones((L, MLA_NOPE), F32)
    zpad = jnp.zeros((L, MLA_HEAD_PAD - MLA_QK), F32)
    znope = jnp.zeros((L, MLA_NOPE), F32)
    c_tab = jnp.concatenate([ones, cos, cos, zpad], axis=-1)
    s_tab = jnp.concatenate([znope, sin, sin, zpad], axis=-1)
    return c_tab, s_tab


def _mla_prep_kernel(x_ref, gcq_ref, gckv_ref, wuq_ref, wuqr_ref, wuk_ref, wuv_ref, vones_ref, gq_ref, gk_ref,
                     c_ref, s_ref, q_ref, k_ref, v_ref):
    lo = MLA_Q_RANK + MLA_KV_RANK
    cq = x_ref[0, :, 0:MLA_Q_RANK].astype(F32)
    ckv = x_ref[0, :, MLA_Q_RANK:lo].astype(F32)
    kr = x_ref[0, :, lo:lo + MLA_HEAD_PAD].astype(F32)
    kr_rot = x_ref[0, :, lo + MLA_HEAD_PAD:].astype(F32)
    cqn = (cq * lax.rsqrt(jnp.mean(cq * cq, axis=-1, keepdims=True) + NORM_EPS) * gcq_ref[...]).astype(BF16)
    ckvn = (ckv * lax.rsqrt(jnp.mean(ckv * ckv, axis=-1, keepdims=True) + NORM_EPS) * gckv_ref[...]).astype(BF16)
    q_raw = jnp.dot(cqn, wuq_ref[...], preferred_element_type=F32)
    q_rot = jnp.dot(cqn, wuqr_ref[...], preferred_element_type=F32)
    k_raw = jnp.dot(ckvn, wuk_ref[...], preferred_element_type=F32)
    v_ref[0] = (jnp.dot(ckvn, wuv_ref[...], preferred_element_type=F32) + vones_ref[...]).astype(BF16)
    s = s_ref[...]
    gqc = gq_ref[...] * c_ref[...]
    gkc = gk_ref[...] * c_ref[...]
    kr_rot_s = kr_rot * s

    def norm_rope(t, gc, rot_s):
        inv = lax.rsqrt(jnp.sum(t * t, axis=-1, keepdims=True) * (1.0 / MLA_QK) + NORM_EPS)
        return (inv * (t * gc + rot_s)).astype(BF16)

    for h in range(MLA_HEADS):
        sl = slice(h * MLA_HEAD_PAD, (h + 1) * MLA_HEAD_PAD)
        q_ref[0, :, sl] = norm_rope(q_raw[:, sl], gqc, q_rot[:, sl] * s)
        k_ref[0, :, sl] = norm_rope(k_raw[:, sl] + kr, gkc, kr_rot_s)


def mla_prep(mla_in, lw, tabs, b, L):
    tm = _tile(L, 512)
    width = MLA_HEADS * MLA_HEAD_PAD
    in_w = mla_in.shape[-1]
    x3 = mla_in.reshape(b, L, in_w)
    tab_spec = pl.BlockSpec((tm, MLA_HEAD_PAD), lambda i, j: (j, 0))
    weights = (lw["gcq"], lw["gckv"], lw["wuq"], lw["wuq_rot"], lw["wuk"], lw["wuv"], lw["v_ones"], lw["gq"], lw["gk"])
    slab_spec = pl.BlockSpec((1, tm, width), lambda i, j: (i, j, 0))
    slab = jax.ShapeDtypeStruct((b, L, width), BF16)
    return pl.pallas_call(
        _mla_prep_kernel,
        grid=(b, L // tm),
        in_specs=[pl.BlockSpec((1, tm, in_w), lambda i, j: (i, j, 0))]
        + [_resident(a) for a in weights] + [tab_spec] * 2,
        out_specs=[slab_spec, slab_spec, slab_spec],
        out_shape=[slab, slab, slab],
        compiler_params=_params("parallel", "parallel"),
        name="mla_prep",
    )(x3, *weights, *tabs)


def _mla_flash_kernel(q_ref, k_ref, v_ref, o_ref, m_ref, acc_ref):
    ki = pl.program_id(2)
    tk = k_ref.shape[1]

    @pl.when(ki == 0)
    def _():
        m_ref[...] = jnp.full(m_ref.shape, -jnp.inf, F32)
        acc_ref[...] = jnp.zeros(acc_ref.shape, F32)

    def scores(h):
        sl = slice(h * MLA_HEAD_PAD, (h + 1) * MLA_HEAD_PAD)
        return lax.dot_general(q_ref[0, :, sl], k_ref[0, :, sl], (((1,), (1,)), ((), ())),
                               preferred_element_type=F32)

    s_next = scores(0)
    for h in range(MLA_HEADS):
        sl = slice(h * MLA_HEAD_PAD, (h + 1) * MLA_HEAD_PAD)
        s = s_next
        if h + 1 < MLA_HEADS:
            s_next = scores(h + 1)
        m_prev = m_ref[h]
        m_next = jnp.maximum(m_prev, jnp.max(s, axis=-1, keepdims=True))
        alpha = jnp.exp2(m_prev - m_next)
        p = jnp.exp2(s - jnp.concatenate([m_next] * (tk // LANES), axis=1))
        acc_ref[h] = alpha * acc_ref[h] + jnp.dot(p.astype(BF16), v_ref[0, :, sl], preferred_element_type=F32)
        m_ref[h] = m_next

    @pl.when(ki == pl.num_programs(2) - 1)
    def _():
        lane = lax.broadcasted_iota(jnp.int32, (acc_ref.shape[1], LANES), 1)

        def normalised(h):
            acc = acc_ref[h]
            return acc / pltpu.roll(acc, MLA_V, 1)

        for hp in range(MLA_HEADS // 2):
            pair = jnp.where(lane < MLA_V, normalised(2 * hp), pltpu.roll(normalised(2 * hp + 1), MLA_V, 1))
            o_ref[0, :, hp * LANES:(hp + 1) * LANES] = pair.astype(o_ref.dtype)


def mla_flash(q, k, v, tq_pref=1024, tk_pref=2048):
    b, L, width = q.shape
    tq = _tile(L, tq_pref)
    tk = _tile(L, tk_pref, LANES)
    return pl.pallas_call(
        _mla_flash_kernel,
        grid=(b, L // tq, L // tk),
        in_specs=[pl.BlockSpec((1, tq, width), lambda bi, qi, ki: (bi, qi, 0)),
                  pl.BlockSpec((1, tk, width), lambda bi, qi, ki: (bi, ki, 0)),
                  pl.BlockSpec((1, tk, width), lambda bi, qi, ki: (bi, ki, 0))],
        out_specs=pl.BlockSpec((1, tq, MLA_WIDTH), lambda bi, qi, ki: (bi, qi, 0)),
        out_shape=jax.ShapeDtypeStruct((b, L, MLA_WIDTH), BF16),
        scratch_shapes=[pltpu.VMEM((MLA_HEADS, tq, LANES), F32),
                        pltpu.VMEM((MLA_HEADS, tq, LANES), F32)],
        compiler_params=_params("parallel", "parallel", "arbitrary"),
        name="mla_flash",
    )(q, k, v)


def _rope_partner(w, g):
    half = MLA_ROPE // 2
    wg = w.astype(F32) * g
    return jnp.concatenate([-wg[..., half:], wg[..., :half]], axis=-1)


def _mla_layer_weights(cq_norm, ckv_norm, w_uq, w_ukv, q_norm, k_norm):
    pad = MLA_HEAD_PAD - MLA_QK
    gq = q_norm.astype(F32) * (MLA_QK ** -0.5 * np.log2(np.e))
    wuq3 = w_uq.reshape(MLA_Q_RANK, MLA_HEADS, MLA_QK)
    wuq = jnp.pad(wuq3, ((0, 0), (0, 0), (0, pad)))
    wuq_rot = jnp.pad(_rope_partner(wuq3[:, :, MLA_NOPE:], gq[MLA_NOPE:]), ((0, 0), (0, 0), (MLA_NOPE, pad)))
    wukv = w_ukv.reshape(MLA_KV_RANK, MLA_HEADS, MLA_NOPE + MLA_V)
    wuk = jnp.pad(wukv[:, :, :MLA_NOPE], ((0, 0), (0, 0), (0, MLA_HEAD_PAD - MLA_NOPE)))
    wuv = jnp.pad(wukv[:, :, MLA_NOPE:], ((0, 0), (0, 0), (0, MLA_HEAD_PAD - MLA_V)))
    v_ones = np.tile(np.arange(MLA_HEAD_PAD) >= MLA_V, MLA_HEADS).astype(np.float32)
    return {
        "v_ones": jnp.asarray(v_ones).reshape(1, -1),
        "gcq": cq_norm.reshape(1, -1).astype(F32),
        "gckv": ckv_norm.reshape(1, -1).astype(F32),
        "wuq": wuq.reshape(MLA_Q_RANK, -1).astype(BF16),
        "wuq_rot": wuq_rot.reshape(MLA_Q_RANK, -1).astype(BF16),
        "wuk": wuk.reshape(MLA_KV_RANK, -1).astype(BF16),
        "wuv": wuv.reshape(MLA_KV_RANK, -1).astype(BF16),
        "gq": jnp.pad(gq, (0, pad)).reshape(1, -1),
        "gk": jnp.pad(k_norm.astype(F32), (0, pad)).reshape(1, -1),
    }


RW_CHUNK = 64
RW_SUB = 16
RW_QUAD = 4
RW_QW = RW_QUAD * RW_HEAD_DIM
RW_SPLITS = (RW_WIDTH, 2 * RW_WIDTH, 3 * RW_WIDTH, 3 * RW_WIDTH + 2 * RW_DECAY_RANK,
             3 * RW_WIDTH + 2 * RW_DECAY_RANK + 2 * RW_A_RANK)
HALO = 8


def _dot3(x, w_hi, w_lo):
    hi, lo = _split2(x)
    return (jnp.dot(hi, w_hi, preferred_element_type=F32) + jnp.dot(lo, w_hi, preferred_element_type=F32)
            + jnp.dot(hi, w_lo, preferred_element_type=F32))


def _chunk_cumsum(x, tri):
    hi = x.astype(BF16)
    r1 = x - hi.astype(F32)
    mid = r1.astype(BF16)
    lo = (r1 - mid.astype(F32)).astype(BF16)
    return (jnp.dot(tri, hi, preferred_element_type=F32) + jnp.dot(tri, mid, preferred_element_type=F32)
            + jnp.dot(tri, lo, preferred_element_type=F32))


def _rw_prep(p, prev_row, next_row, mu_ref, wup_hi, wup_lo, w0_ref, aup_hi, aup_lo, a0_ref,
             gup_hi, gup_lo, kk_ref, ka_ref, ones_ref, trif_ref, trib_ref,
             r_ref, v_ref, kkn_ref, g_ref, lw_ref, kd_ref, a_ref, cum_ref):
    tm = p.shape[0]
    row = lax.broadcasted_iota(jnp.int32, p.shape, 0)
    prev = jnp.where(row == 0, prev_row, pltpu.roll(p, 1, 0))
    nxt = jnp.where(row == tm - 1, next_row, pltpu.roll(p, tm - 1, 0))
    pm = p + mu_ref[...] * (0.5 * (prev + nxt) - p)
    s0, s1, s2, s3, s4 = RW_SPLITS
    r, k, v = pm[:, 0:s0], pm[:, s0:s1], pm[:, s1:s2]
    wd = jnp.tanh(pm[:, s2:s3])
    ad = pm[:, s3:s4]
    gd = pm[:, s4:]
    w_raw = w0_ref[...] + _dot3(wd, wup_hi[...], wup_lo[...])
    lw = -np.float32(np.exp(-0.5)) * jax.nn.sigmoid(w_raw)
    a = jax.nn.sigmoid(a0_ref[...] + _dot3(ad, aup_hi[...], aup_lo[...]))
    g_ref[...] = _dot3(jax.nn.sigmoid(gd), gup_hi[...], gup_lo[...])
    kk = k * kk_ref[...]
    kkn_ref[...] = kk * lax.rsqrt(_seg_sum(kk * kk, ones_ref[...]) + 1e-12)
    for d, tri_ref in enumerate((trif_ref, trib_ref)):
        sl = slice(d * RW_WIDTH, (d + 1) * RW_WIDTH)
        lw_ref[d] = lw[:, sl]
        cum_ref[d] = _chunk_cumsum(lw[:, sl], tri_ref[...])
        a_ref[d] = a[:, sl]
        kd_ref[d] = k * (1.0 + (a[:, sl] - 1.0) * ka_ref[...])
    r_ref[...] = r
    v_ref[...] = v


N_RW_PREP_WEIGHTS = 14
N_RW_PREP_OUTPUTS = 8


def _in_proj_rw_kernel(x_ref, xp_ref, xn_ref, g_ref, wna_ref, wmla_ref, wgate_ref, wrw_ref, *refs, tiles_per_seq):
    rw_weights = refs[:N_RW_PREP_WEIGHTS]
    ona_ref, omla_ref, ogate_ref = refs[N_RW_PREP_WEIGHTS:N_RW_PREP_WEIGHTS + 3]
    rw_outs = refs[N_RW_PREP_WEIGHTS + 3:]
    pos = pl.program_id(0) % tiles_per_seq

    def normed(x):
        return (x * lax.rsqrt(jnp.mean(x * x, axis=-1, keepdims=True) + NORM_EPS) * g_ref[...]).astype(BF16)

    h = normed(x_ref[...])
    ona_ref[...] = jnp.dot(h, wna_ref[...], preferred_element_type=F32).astype(ona_ref.dtype)
    omla_ref[...] = jnp.dot(h, wmla_ref[...], preferred_element_type=F32).astype(omla_ref.dtype)
    ogate_ref[...] = jnp.dot(h, wgate_ref[...], preferred_element_type=F32).astype(ogate_ref.dtype)
    p = jnp.dot(h, wrw_ref[...], preferred_element_type=F32)
    halo = normed(jnp.concatenate([xp_ref[...], xn_ref[...]], axis=0))
    p_halo = jnp.dot(halo, wrw_ref[...], preferred_element_type=F32)
    prev_row = jnp.where(pos > 0, p_halo[HALO - 1:HALO], 0.0)
    next_row = jnp.where(pos < tiles_per_seq - 1, p_halo[HALO:HALO + 1], 0.0)
    _rw_prep(p, prev_row, next_row, *rw_weights, *rw_outs)


def in_proj_rw(x, g, w_na, w_mla, w_gate, w_rw, lw, L):
    t, d = x.shape
    tm = _tile(L, 256, RW_CHUNK)
    nh = t // HALO
    g = g.reshape(1, d)
    t_i = np.arange(tm)[:, None]
    s_i = np.arange(tm)[None, :]
    same_chunk = (t_i // RW_CHUNK) == (s_i // RW_CHUNK)
    tri_f = jnp.asarray(same_chunk & (s_i <= t_i), BF16)
    tri_b = jnp.asarray(same_chunk & (s_i >= t_i), BF16)
    rw_weights = (lw["mu"], lw["wup_hi"], lw["wup_lo"], lw["w0"], lw["aup_hi"], lw["aup_lo"], lw["a0"],
                  lw["gup_hi"], lw["gup_lo"], lw["k_k"], lw["k_a"], lw["ones_bd"], tri_f, tri_b)
    assert len(rw_weights) == N_RW_PREP_WEIGHTS
    proj = (w_na, w_mla, w_gate)
    one = jax.ShapeDtypeStruct((t, RW_WIDTH), F32)
    two = jax.ShapeDtypeStruct((2, t, RW_WIDTH), F32)
    one_spec = pl.BlockSpec((tm, RW_WIDTH), lambda i: (i, 0))
    two_spec = pl.BlockSpec((2, tm, RW_WIDTH), lambda i: (0, i, 0))
    return pl.pallas_call(
        functools.partial(_in_proj_rw_kernel, tiles_per_seq=L // tm),
        grid=(t // tm,),
        in_specs=[pl.BlockSpec((tm, d), lambda i: (i, 0)),
                  pl.BlockSpec((HALO, d), lambda i: (jnp.maximum(i * (tm // HALO) - 1, 0), 0)),
                  pl.BlockSpec((HALO, d), lambda i: (jnp.minimum((i + 1) * (tm // HALO), nh - 1), 0)),
                  _resident(g)] + [_resident(w) for w in proj + (w_rw,) + rw_weights],
        out_specs=[pl.BlockSpec((tm, w.shape[1]), lambda i: (i, 0)) for w in proj]
        + [one_spec] * 4 + [two_spec] * 4,
        out_shape=[jax.ShapeDtypeStruct((t, w.shape[1]), BF16) for w in proj] + [one] * 4 + [two] * 4,
        compiler_params=_params("parallel"),
        name="in_proj_rw",
    )(x, x, x, g, *proj, w_rw, *rw_weights)


def _rw_masks(reverse):
    C, NQ = RW_CHUNK, RW_QUAD
    wt = lax.broadcasted_iota(jnp.int32, (C, NQ * C), 0)
    ws = lax.broadcasted_iota(jnp.int32, (C, NQ * C), 1) & (C - 1)
    strict = (ws > wt) if reverse else (ws < wt)
    incl = (ws >= wt) if reverse else (ws <= wt)
    same = (ws // RW_SUB) == (wt // RW_SUB)
    eye = jnp.where(ws == wt, 1.0, 0.0)
    return strict, incl, same, eye


def _rw_chunks(chains, bdm, bdm_b):
    C, NQ = RW_CHUNK, RW_QUAD
    n = len(chains)
    masks = {rev: _rw_masks(rev) for rev in sorted({c["reverse"] for c in chains})}
    strict = [masks[c["reverse"]][0] for c in chains]
    incl = [masks[c["reverse"]][1] for c in chains]
    same = [masks[c["reverse"]][2] for c in chains]
    eye = [masks[c["reverse"]][3] for c in chains]
    ids = range(n)

    def dot(a, b):
        return jnp.dot(a, b, preferred_element_type=F32)

    def dot_nt(a, b):
        return lax.dot_general(a, b, (((1,), (1,)), ((), ())), preferred_element_type=F32)

    def bd(x):
        return jnp.concatenate([x] * NQ, axis=0) * bdm_b

    def mm(ms, xs):
        return [dot(m.astype(BF16), bd(x.astype(BF16))) for m, x in zip(ms, xs)]

    lw = [c["lw"] for c in chains]
    cum = [c["cum"] for c in chains]
    tot =[cum[i][0:1] if chains[i]["reverse"] else cum[i][C - 1:C] for i in ids]
    pinv = [jnp.exp(-x) for x in cum]
    pend = [jnp.exp(t - x) for t, x in zip(tot, cum)]
    kka = [c["kk"] * c["a"] for c in chains]
    kap = [(chains[i]["kk"] * jnp.exp(cum[i] - lw[i])).astype(BF16) for i in ids]
    bet = [(kka[i] * pinv[i]).astype(BF16) for i in ids]
    kt = [(chains[i]["kd"] * pinv[i]).astype(BF16) for i in ids]
    rt = [(chains[i]["r"] * jnp.exp(cum[i])).astype(BF16) for i in ids]
    v = [c["v"] for c in chains]

    x2 = [jnp.concatenate([kap[i], rt[i]], axis=0) for i in ids]
    y2 = [jnp.concatenate([bd(bet[i]), bd(kt[i])], axis=0) for i in ids]
    aw = [dot_nt(x2[i], y2[i]) for i in ids]
    zz = [dot_nt(x2[i], chains[i]["zt"].astype(BF16)) for i in ids]
    a_ab = [jnp.where(strict[i], aw[i][:C, :NQ * C], 0.0) for i in ids]
    a_ak = [jnp.where(strict[i], aw[i][:C, NQ * C:], 0.0) for i in ids]
    a_rb = [jnp.where(incl[i], aw[i][C:, :NQ * C], 0.0) for i in ids]
    a_rk = [jnp.where(incl[i], aw[i][C:, NQ * C:], 0.0) for i in ids]
    def mm2(tops, bottoms, xs):
        both = mm([jnp.concatenate([a, b], axis=0) for a, b in zip(tops, bottoms)], xs)
        return [x[:C] for x in both], [x[C:] for x in both]

    akv, yv = mm2(a_ak, a_rk, v)
    rhs = [-(zz[i][:C] + akv[i]) for i in ids]
    d = [jnp.where(same[i], a_ab[i], 0.0) for i in ids]
    e = [a_ab[i] - d[i] for i in ids]
    d2 = mm(d, d)
    t = [eye[i] - d[i] for i in ids]
    d4, td = mm2(d2, t, d2)
    t = [x + y for x, y in zip(t, td)]
    d8, td = mm2(d4, t, d4)
    t = [x + y for x, y in zip(t, td)]
    t = [x + y for x, y in zip(t, mm(t, d8))]
    nn = mm(t, e)
    n2 = mm(nn, nn)
    u = mm(t, rhs)
    u = [x + y for x, y in zip(u, mm(n2, u))]
    u = [x - y for x, y in zip(u, mm(nn, u))]
    yu = mm(a_rb, u)
    y = [zz[i][C:] + yv[i] + yu[i] for i in ids]
    lhs = [jnp.concatenate([v[i], u[i]], axis=0).astype(BF16) for i in ids]
    rhs2 = [jnp.concatenate([chains[i]["kd"] * pend[i], kka[i] * pend[i]], axis=0).astype(BF16) for i in ids]
    upd = [lax.dot_general(lhs[i], rhs2[i], (((0,), (0,)), ((), ())), preferred_element_type=F32) for i in ids]
    zt = [chains[i]["zt"] * jnp.exp(tot[i]) + upd[i] * bdm for i in ids]
    return list(zip(y, zt))


RW_SEQS = 4


def _rw_scan_kernel(rf, vf, kkf, lwf, kdf, af, cf, rb, vb, kkb, lwb, kdb, ab, cb, bdm_ref, yf_ref, yb_ref, zt_ref):
    @pl.when(pl.program_id(1) == 0)
    def _():
        zt_ref[...] = jnp.zeros(zt_ref.shape, F32)

    bdm = bdm_ref[...]
    nq = RW_HEADS // RW_QUAD
    names = ("r", "lw", "kd", "v", "kk", "a", "cum")
    chains, outs = [], []
    for bi in range(RW_SEQS):
        for di, (reverse, refs, y_ref) in enumerate(((False, (rf, lwf, kdf, vf, kkf, af, cf), yf_ref),
                                                     (True, (rb, lwb, kdb, vb, kkb, ab, cb), yb_ref))):
            for q in range(nq):
                sl = slice(q * RW_QW, (q + 1) * RW_QW)
                idx = (bi * 2 + di) * nq + q
                chain = {name: ref[bi, :, sl] for name, ref in zip(names, refs)}
                chain["zt"] = zt_ref[idx]
                chain["reverse"] = reverse
                chains.append(chain)
                outs.append((y_ref, bi, sl, idx))
    for (y, zt), (y_ref, bi, sl, idx) in zip(_rw_chunks(chains, bdm, bdm.astype(BF16)), outs):
        y_ref[bi, :, sl] = y
        zt_ref[idx] = zt


def rw_scan(r, v, kk, lw, kd, a, cum, bdm):
    b, L, _ = r.shape
    assert b % RW_SEQS == 0
    nc = L // RW_CHUNK
    blk = (RW_SEQS, RW_CHUNK, RW_WIDTH)
    f1 = pl.BlockSpec(blk, lambda bi, c: (bi, c, 0))
    b1 = pl.BlockSpec(blk, lambda bi, c: (bi, nc - 1 - c, 0))
    f2 = pl.BlockSpec((None,) + blk, lambda bi, c: (0, bi, c, 0))
    b2 = pl.BlockSpec((None,) + blk, lambda bi, c: (1, bi, nc - 1 - c, 0))
    out = jax.ShapeDtypeStruct((b, L, RW_WIDTH), F32)
    return pl.pallas_call(
        _rw_scan_kernel,
        grid=(b // RW_SEQS, nc),
        in_specs=[f1, f1, f1, f2, f2, f2, f2, b1, b1, b1, b2, b2, b2, b2, _resident(bdm)],
        out_specs=[f1, b1],
        out_shape=[out, out],
        scratch_shapes=[pltpu.VMEM((RW_SEQS * 2 * RW_HEADS // RW_QUAD, RW_QW, RW_QW), F32)],
        compiler_params=_params("parallel", "arbitrary"),
        name="rw_scan",
    )(r, v, kk, lw, kd, a, cum, r, v, kk, lw, kd, a, cum, bdm)


def _rw_post(yf_ref, yb_ref, r_ref, v_ref, kdf_ref, kdb_ref, g_ref, lnw_ref, lnb_ref, rk_ref, ones_ref):
    ones_bd = ones_ref[...]
    inv_n = 1.0 / RW_HEAD_DIM
    y = yf_ref[...] + yb_ref[...]
    yc = y - _seg_sum(y, ones_bd) * inv_n
    var = _seg_sum(yc * yc, ones_bd) * inv_n
    y = yc * lax.rsqrt(var + RW_LN_EPS) * lnw_ref[...] + lnb_ref[...]
    kd = kdf_ref[...] + kdb_ref[...]
    bonus = _seg_sum(r_ref[...] * kd * rk_ref[...], ones_bd)
    return ((y + bonus * v_ref[...]) * g_ref[...]).astype(BF16)


def _block_diag2(w):
    z = jnp.zeros_like(w[0])
    return jnp.concatenate([jnp.concatenate([w[0], z], axis=1), jnp.concatenate([z, w[1]], axis=1)], axis=0)


def _hi_lo(w):
    w = w.astype(F32)
    hi = w.astype(BF16)
    return hi, (w - hi.astype(F32)).astype(BF16)


def _rw_layer_weights(mu, w0, w_up, a0, a_up, g_up, k_k, k_a, r_k, ln_w, ln_b):
    wup_hi, wup_lo = _hi_lo(_block_diag2(w_up))
    aup_hi, aup_lo = _hi_lo(_block_diag2(a_up))
    gup_hi, gup_lo = _hi_lo(g_up)

    def row(t):
        return t.reshape(1, -1).astype(F32)

    return {
        "mu": row(mu), "wup_hi": wup_hi, "wup_lo": wup_lo, "w0": row(w0),
        "aup_hi": aup_hi, "aup_lo": aup_lo, "a0": row(a0), "gup_hi": gup_hi, "gup_lo": gup_lo,
        "k_k": row(k_k), "k_a": row(k_a), "r_k": row(r_k), "ln_w": row(ln_w), "ln_b": row(ln_b),
        "ones_bd": _head_ones(RW_WIDTH, RW_HEAD_DIM),
    }


def rwkv7_mix(prepared, lw, bdm, b, L):
    r, v, kk, g, lwd, kd, a, cum = prepared
    t = b * L

    def seq(x):
        return x.reshape(x.shape[:-2] + (b, L, RW_WIDTH))

    yf, yb = rw_scan(seq(r), seq(v), seq(kk), seq(lwd), seq(kd), seq(a), seq(cum), bdm)
    return (yf.reshape(t, RW_WIDTH), yb.reshape(t, RW_WIDTH), r, v, kd, g)


N_RW_POST_REFS = 11


def _merge_kernel(x_ref, ya_ref, yb_ref, *refs):
    rw_refs = refs[:N_RW_POST_REFS]
    gin_ref, bg_ref, pa_ref, pb_ref, pc_ref, wo_ref, o_ref = refs[N_RW_POST_REFS:]
    branches = ((ya_ref[...], pa_ref), (yb_ref[...], pb_ref), (_rw_post(*rw_refs), pc_ref))
    mixed = None
    for i, (y, p_ref) in enumerate(branches):
        sl = slice(i * D_MODEL, (i + 1) * D_MODEL)
        gate = jax.nn.sigmoid(gin_ref[:, sl].astype(F32) + bg_ref[:, sl])
        term = gate * jnp.dot(y, p_ref[...], preferred_element_type=F32)
        mixed = term if mixed is None else mixed + term
    o_ref[...] = x_ref[...] + jnp.dot(mixed.astype(BF16), wo_ref[...], preferred_element_type=F32)


def merge(x, ya, yb, rw_parts, rw_lw, gate_in, b_gate, pa, pb, pc, wo):
    t = x.shape[0]
    tm = _tile(t, 512)
    yf, ybk, r, v, kd, g = rw_parts

    def rows(w):
        return pl.BlockSpec((tm, w), lambda i: (i, 0))

    one = rows(RW_WIDTH)
    kd_f = pl.BlockSpec((None, tm, RW_WIDTH), lambda i: (0, i, 0))
    kd_b = pl.BlockSpec((None, tm, RW_WIDTH), lambda i: (1, i, 0))
    rw_weights = (rw_lw["ln_w"], rw_lw["ln_b"], rw_lw["r_k"], rw_lw["ones_bd"])
    weights = (b_gate, pa, pb, pc, wo)
    return pl.pallas_call(
        _merge_kernel,
        grid=(t // tm,),
        in_specs=[rows(D_MODEL), rows(NA_WIDTH), rows(MLA_WIDTH), one, one, one, one, kd_f, kd_b, one]
        + [_resident(a) for a in rw_weights] + [rows(N_BRANCH * D_MODEL)] + [_resident(a) for a in weights],
        out_specs=rows(D_MODEL),
        out_shape=jax.ShapeDtypeStruct((t, D_MODEL), F32),
        compiler_params=_params("parallel"),
        name="merge",
    )(x, ya, yb, yf, ybk, r, v, kd, kd, g, *rw_weights, gate_in, *weights)


def _ffn_kernel(x_ref, g_ref, wg_ref, wu_ref, wd_ref, o_ref):
    x = x_ref[...]
    ms = jnp.mean(x * x, axis=-1, keepdims=True)
    h = (x * lax.rsqrt(ms + NORM_EPS) * g_ref[...]).astype(BF16)
    gate = jnp.dot(h, wg_ref[...], preferred_element_type=F32)
    up = jnp.dot(h, wu_ref[...], preferred_element_type=F32)
    act = (gate * jax.nn.sigmoid(gate) * up).astype(BF16)
    o_ref[...] = x + jnp.dot(act, wd_ref[...], preferred_element_type=F32)


def ffn(x, g, wg, wu, wd, tm_pref=512):
    t, d = x.shape
    tm = _tile(t, tm_pref)
    g = g.reshape(1, d)
    return pl.pallas_call(
        _ffn_kernel,
        grid=(t // tm,),
        in_specs=[pl.BlockSpec((tm, d), lambda i: (i, 0))] + [_resident(a) for a in (g, wg, wu, wd)],
        out_specs=pl.BlockSpec((tm, d), lambda i: (i, 0)),
        out_shape=jax.ShapeDtypeStruct((t, d), F32),
        compiler_params=_params("parallel"),
        name="ffn",
    )(x, g, wg, wu, wd)


IN_SIZES = (NA_WIDTH, NA_WIDTH, NA_WIDTH, MLA_Q_RANK, MLA_KV_RANK, MLA_ROPE, RW_IN, N_BRANCH * D_MODEL)
IN_SPLITS = tuple(int(s) for s in np.cumsum(IN_SIZES)[:-1])


def _layer_weights(l, norm1_g, w_in, b_gate, na_q_norm, na_k_norm, na_rpb, na_proj,
                   mla_cq_norm, mla_ckv_norm, mla_w_uq, mla_w_ukv, mla_q_norm, mla_k_norm, mla_proj,
                   rw_mu, rw_w0, rw_w_up, rw_a0, rw_a_up, rw_g_up, rw_k_k, rw_k_a, rw_r_k, rw_ln_w, rw_ln_b, rw_proj,
                   w_out, norm2_g, ffn_w_gate, ffn_w_up, ffn_w_down):
    w = w_in[l]
    slab_pad = ((0, 0), (MLA_NOPE, MLA_HEAD_PAD - MLA_QK))
    w_kr = w[:, IN_SPLITS[4]:IN_SPLITS[5]]
    w_kr_rot = _rope_partner(w_kr, mla_k_norm[l].astype(F32)[MLA_NOPE:])
    w_mla = jnp.concatenate([w[:, IN_SPLITS[2]:IN_SPLITS[4]], jnp.pad(w_kr, slab_pad), jnp.pad(w_kr_rot, slab_pad)],
                            axis=1)
    return {
        "norm1_g": norm1_g[l], "w_na": w[:, :IN_SPLITS[2]].astype(BF16), "w_mla": w_mla.astype(BF16),
        "w_rw": w[:, IN_SPLITS[5]:IN_SPLITS[6]].astype(BF16),
        "w_gate": w[:, IN_SPLITS[6]:].astype(BF16),
        "b_gate": b_gate[l].reshape(1, -1).astype(F32),
        "na_gq": na_q_norm[l].astype(F32) * (NA_HEAD_DIM ** -0.5), "na_gk": na_k_norm[l].astype(F32),
        "na_bias": _na_bias_table(na_rpb[l]),
        "na_proj": na_proj[l].astype(BF16), "mla_proj": mla_proj[l].astype(BF16), "rw_proj": rw_proj[l].astype(BF16),
        "mla": _mla_layer_weights(mla_cq_norm[l], mla_ckv_norm[l], mla_w_uq[l], mla_w_ukv[l], mla_q_norm[l], mla_k_norm[l]),
        "rw": _rw_layer_weights(rw_mu[l], rw_w0[l], rw_w_up[l], rw_a0[l], rw_a_up[l], rw_g_up[l],
                                rw_k_k[l], rw_k_a[l], rw_r_k[l], rw_ln_w[l], rw_ln_b[l]),
        "w_out": w_out[l].astype(BF16), "norm2_g": norm2_g[l],
        "ffn_w_gate": ffn_w_gate[l].astype(BF16), "ffn_w_up": ffn_w_up[l].astype(BF16),
        "ffn_w_down": ffn_w_down[l].astype(BF16),
    }


def _trunk(x, layers):
    b, L, d = x.shape
    t = b * L
    x = x.reshape(t, d)
    tabs = _rope_tables(L)
    bdm = _head_ones(RW_QW, RW_HEAD_DIM).astype(F32)
    for lw in layers:
        na_in, mla_in, gate_in, *rw_prepared = in_proj_rw(x, lw["norm1_g"], lw["w_na"], lw["w_mla"], lw["w_gate"],
                                                          lw["w_rw"], lw["rw"], L)
        qa, ka, va = na_prep(na_in, lw["na_gq"], lw["na_gk"], b, L)
        y_a = na_attention(qa, ka, va, lw["na_bias"]).reshape(t, NA_WIDTH)
        qm, km, vm = mla_prep(mla_in, lw["mla"], tabs, b, L)
        y_b = mla_flash(qm, km, vm).reshape(t, MLA_WIDTH)
        y_c = rwkv7_mix(rw_prepared, lw["rw"], bdm, b, L)
        x = merge(x, y_a, y_b, y_c, lw["rw"], gate_in, lw["b_gate"], lw["na_proj"], lw["mla_proj"], lw["rw_proj"],
                  lw["w_out"])
        x = ffn(x, lw["norm2_g"], lw["ffn_w_gate"], lw["ffn_w_up"], lw["ffn_w_down"])
    return x.reshape(b, L, d)


def kernel(x_prompt, x_sample, norm1_g, w_in, b_gate, na_q_norm, na_k_norm, na_rpb, na_proj, mla_cq_norm, mla_ckv_norm, mla_w_uq, mla_w_ukv, mla_q_norm, mla_k_norm, mla_proj, rw_mu, rw_w0, rw_w_up, rw_a0, rw_a_up, rw_g_up, rw_k_k, rw_k_a, rw_r_k, rw_ln_w, rw_ln_b, rw_proj, w_out, norm2_g, ffn_w_gate, ffn_w_up, ffn_w_down):
    weights = (norm1_g, w_in, b_gate, na_q_norm, na_k_norm, na_rpb, na_proj,
               mla_cq_norm, mla_ckv_norm, mla_w_uq, mla_w_ukv, mla_q_norm, mla_k_norm, mla_proj,
               rw_mu, rw_w0, rw_w_up, rw_a0, rw_a_up, rw_g_up, rw_k_k, rw_k_a, rw_r_k, rw_ln_w, rw_ln_b, rw_proj,
               w_out, norm2_g, ffn_w_gate, ffn_w_up, ffn_w_down)
    layers = [_layer_weights(l, *weights) for l in range(norm1_g.shape[0])]
    return (_trunk(x_prompt, layers), _trunk(x_sample, layers))
```
